```python
import jax, jax.numpy as jnp
from jax import lax
import numpy as np

D_MODEL = 2048
BATCH = 4
SEQ = 2048
DEPTH = 1
DEC_BATCH = 1
DEC_SEQ = 16384
PAST_LEN = 128

N_META = 16
GRID_W = 64
CHUNK = 128
Q_BLOCK = 128
MIX_W = D_MODEL
RET_HEADS = 4
RET_V = MIX_W // 2
RET_DV = RET_V // RET_HEADS
RET_DK = RET_DV // 2
RET_QK = RET_HEADS * RET_DK
ATT_DH = 128
ATT_HEADS = (MIX_W - RET_V) // ATT_DH
ATT_KV_HEADS = ATT_HEADS // 4
ATT_Q = ATT_HEADS * ATT_DH
ATT_KV = ATT_KV_HEADS * ATT_DH
D_FF = 4 * D_MODEL
ROPE_THETA = 10000.0
EPS = 1e-6
IN_SPLITS = (RET_QK, RET_QK, RET_V, RET_V, ATT_Q, ATT_KV, ATT_KV)
IN_W = sum(IN_SPLITS)
SPLIT_IDX = [int(i) for i in np.cumsum(IN_SPLITS)[:-1]]

kernel_name = "hymba_retention_axial_gqa_encoder"


def rms_norm(x, g=None):
    xf = x.astype(jnp.float32)
    y = xf * lax.rsqrt(jnp.mean(xf * xf, axis=-1, keepdims=True) + EPS)
    if g is not None:
        y = y * g.astype(jnp.float32)
    return y.astype(x.dtype)


def rope_pairs(x, ang):
    xf = x.astype(jnp.float32)
    x1, x2 = xf[..., 0::2], xf[..., 1::2]
    c, s = jnp.cos(ang), jnp.sin(ang)
    y = jnp.stack([x1 * c - x2 * s, x1 * s + x2 * c], axis=-1).reshape(x.shape)
    return y.astype(x.dtype)


def axial_angles(n_tokens):
    rows_n = n_tokens // GRID_W
    row = jnp.repeat(jnp.arange(rows_n), GRID_W).astype(jnp.float32)
    col = jnp.tile(jnp.arange(GRID_W), rows_n).astype(jnp.float32)
    n_pair = ATT_DH // 4
    freq = ROPE_THETA ** (-jnp.arange(n_pair, dtype=jnp.float32) / n_pair)
    ang = jnp.concatenate([row[:, None] * freq[None], col[:, None] * freq[None]], axis=-1)
    return jnp.concatenate([jnp.zeros((N_META, ATT_DH // 2), jnp.float32), ang], axis=0)


def retention_dir(q, k, v, log_g, inclusive):
    B, H, Lp, dk = q.shape
    dv = v.shape[-1]
    n = Lp // CHUNK
    f32 = jnp.float32
    qc = q.reshape(B, H, n, CHUNK, dk).astype(f32)
    kc = k.reshape(B, H, n, CHUNK, dk).astype(f32)
    vc = v.reshape(B, H, n, CHUNK, dv).astype(f32)
    lg = log_g.astype(f32)
    idx = jnp.arange(CHUNK, dtype=f32)
    diff = idx[:, None] - idx[None, :]
    mask = (diff >= 0) if inclusive else (diff > 0)
    dmat = jnp.where(mask[None], jnp.exp(lg[:, None, None] * jnp.maximum(diff, 0.0)[None]), 0.0)
    scores = jnp.einsum('bhncd,bhnsd->bhncs', qc, kc) * dmat[:, None]
    inner = jnp.einsum('bhncs,bhnsv->bhncv', scores, vc)
    k_dec = jnp.exp(lg[:, None] * (CHUNK - 1 - idx)[None])
    kv = jnp.einsum('bhncd,bhncv,hc->nbhdv', kc, vc, k_dec)
    chunk_decay = jnp.exp(lg * CHUNK)[None, :, None, None]

    def step(s, kv_n):
        return s * chunk_decay + kv_n, s

    _, s_prev = lax.scan(step, jnp.zeros((B, H, dk, dv), f32), kv)
    q_dec = jnp.exp(lg[:, None] * (idx + 1.0)[None])
    cross = jnp.einsum('bhncd,hc,nbhdv->bhncv', qc, q_dec, s_prev)
    return (inner + cross).reshape(B, H, Lp, dv)


def retention_group(q, k, v, g, log_g_fwd, log_g_bwd):
    B, L, _ = q.shape
    pad = CHUNK - N_META

    def heads(t, d):
        t = t.reshape(B, L, RET_HEADS, d).transpose(0, 2, 1, 3)
        return jnp.pad(t, ((0, 0), (0, 0), (pad, 0), (0, 0)))

    qh = heads(q, RET_DK)
    kh = heads(k, RET_DK) * (RET_DK ** -0.5)
    vh = heads(v, RET_DV)
    lp = L + pad
    pos = jnp.arange(lp, dtype=jnp.float32)
    freq = ROPE_THETA ** (-jnp.linspace(0.0, 1.0, RET_DK // 2, dtype=jnp.float32))
    ang = pos[:, None] * freq[None]
    qh = rope_pairs(qh, ang)
    kh = rope_pairs(kh, ang)
    fwd = retention_dir(qh, kh, vh, log_g_fwd, True)
    rev = lambda t: jnp.flip(t, axis=2)
    bwd = rev(retention_dir(rev(qh), rev(kh), rev(vh), log_g_bwd, False))
    o = rms_norm((fwd + bwd)[:, :, pad:])
    o = o.transpose(0, 2, 1, 3).reshape(B, L, RET_V)
    return (o * jax.nn.silu(g.astype(jnp.float32))).astype(q.dtype)


def attention_group(q, k, v, q_g, k_g, ang):
    B, L, _ = q.shape
    G = ATT_HEADS // ATT_KV_HEADS
    qh = q.reshape(B, L, ATT_KV_HEADS, G, ATT_DH).transpose(0, 2, 3, 1, 4)
    kh = k.reshape(B, L, ATT_KV_HEADS, ATT_DH).transpose(0, 2, 1, 3)
    vh = v.reshape(B, L, ATT_KV_HEADS, ATT_DH).transpose(0, 2, 1, 3)
    qh = rope_pairs(rms_norm(qh, q_g), ang) * (ATT_DH ** -0.5)
    kh = rope_pairs(rms_norm(kh, k_g), ang)

    def attend(qb):
        s = jnp.einsum('bkgqd,bksd->bkgqs', qb, kh).astype(jnp.float32)
        p = jax.nn.softmax(s, axis=-1)
        return jnp.einsum('bkgqs,bksd->bkgqd', p.astype(vh.dtype), vh)

    o_meta = attend(qh[:, :, :, :N_META])
    S = L - N_META
    nb = S // Q_BLOCK
    qb = qh[:, :, :, N_META:].reshape(B, ATT_KV_HEADS, G, nb, Q_BLOCK, ATT_DH).transpose(3, 0, 1, 2, 4, 5)
    o_real = lax.map(attend, qb)
    o_real = o_real.transpose(1, 2, 3, 0, 4, 5).reshape(B, ATT_KV_HEADS, G, S, ATT_DH)
    o = jnp.concatenate([o_meta, o_real], axis=3)
    return o.transpose(0, 3, 1, 2, 4).reshape(B, L, ATT_Q)


def encoder_layer(h, ang, ln1, w_in, q_g, k_g, dec_f, dec_b, w_out, ln2, w_up, w_down):
    u = rms_norm(h, ln1)
    proj = jnp.einsum('bld,de->ble', u, w_in)
    rq, rk, rv, rg, aq, ak, av = jnp.split(proj, SPLIT_IDX, axis=-1)
    mix = jnp.concatenate([retention_group(rq, rk, rv, rg, dec_f, dec_b),
                           attention_group(aq, ak, av, q_g, k_g, ang)], axis=-1)
    h = h + jnp.einsum('ble,ed->bld', mix, w_out)
    u = rms_norm(h, ln2)
    a = jnp.square(jax.nn.relu(jnp.einsum('bld,df->blf', u, w_up)))
    return h + jnp.einsum('blf,fd->bld', a, w_down)


def encode(x, meta_tokens, ln1_g, w_in, q_norm_g, k_norm_g, ret_log_decay_fwd, ret_log_decay_bwd,
           w_out, ln2_g, w_up, w_down, final_norm_g):
    B, S, _ = x.shape
    h = jnp.concatenate([jnp.broadcast_to(meta_tokens[None].astype(x.dtype), (B, N_META, D_MODEL)), x], axis=1)
    ang = axial_angles(S)
    for l in range(DEPTH):
        h = encoder_layer(h, ang, ln1_g[l], w_in[l], q_norm_g[l], k_norm_g[l],
                          ret_log_decay_fwd[l], ret_log_decay_bwd[l], w_out[l], ln2_g[l], w_up[l], w_down[l])
    return rms_norm(h[:, N_META:], final_norm_g)


def setup_inputs(seed: int = 0) -> dict:
    key = jax.random.key(seed)
    ks = jax.random.split(key, 16)
    f32 = jnp.float32

    def gain(k, shape):
        return 1.0 + 0.02 * jax.random.normal(k, shape, f32)

    def log_decay(k):
        u = jax.random.uniform(k, (DEPTH, RET_HEADS), f32, 0.0, 0.25)
        e = 5.0 + jnp.arange(RET_HEADS, dtype=f32)[None] + u
        return jnp.log(1.0 - 2.0 ** (-e))

    return {
        "x_prompt": jax.random.normal(ks[0], (BATCH, SEQ, D_MODEL), f32),
        "x_sample": jax.random.normal(ks[1], (DEC_BATCH, DEC_SEQ, D_MODEL), f32),
        "meta_tokens": jax.random.normal(ks[2], (N_META, D_MODEL), f32),
        "ln1_g": gain(ks[3], (DEPTH, D_MODEL)),
        "w_in": jax.random.normal(ks[4], (DEPTH, D_MODEL, IN_W), f32) * D_MODEL ** -0.5,
        "q_norm_g": gain(ks[5], (DEPTH, ATT_DH)),
        "k_norm_g": gain(ks[6], (DEPTH, ATT_DH)),
        "ret_log_decay_fwd": log_decay(ks[7]),
        "ret_log_decay_bwd": log_decay(ks[8]),
        "w_out": jax.random.normal(ks[9], (DEPTH, MIX_W, D_MODEL), f32) * MIX_W ** -0.5,
        "ln2_g": gain(ks[10], (DEPTH, D_MODEL)),
        "w_up": jax.random.normal(ks[11], (DEPTH, D_MODEL, D_FF), f32) * D_MODEL ** -0.5,
        "w_down": jax.random.normal(ks[12], (DEPTH, D_FF, D_MODEL), f32) * D_FF ** -0.5,
        "final_norm_g": gain(ks[13], (D_MODEL,)),
    }


def reference(x_prompt, x_sample, meta_tokens, ln1_g, w_in, q_norm_g, k_norm_g, ret_log_decay_fwd,
              ret_log_decay_bwd, w_out, ln2_g, w_up, w_down, final_norm_g):
    y_prompt = encode(x_prompt, meta_tokens, ln1_g, w_in, q_norm_g, k_norm_g, ret_log_decay_fwd,
                      ret_log_decay_bwd, w_out, ln2_g, w_up, w_down, final_norm_g)
    y_sample = encode(x_sample, meta_tokens, ln1_g, w_in, q_norm_g, k_norm_g, ret_log_decay_fwd,
                      ret_log_decay_bwd, w_out, ln2_g, w_up, w_down, final_norm_g)
    return (y_prompt, y_sample)
```

```python
import functools

import jax
import jax.numpy as jnp
from jax import lax
from jax.experimental import pallas as pl
from jax.experimental.pallas import tpu as pltpu

F32 = jnp.float32
BF16 = jnp.bfloat16

D_MODEL = 2048
N_META = 16
GRID_W = 64
CHUNK = 128
RET_HEADS = 4
RET_DK = 128
RET_DV = 256
RET_QK = RET_HEADS * RET_DK
RET_V = RET_HEADS * RET_DV
ATT_DH = 128
ATT_HEADS = 8
ATT_KV_HEADS = 2
ATT_GROUP = ATT_HEADS // ATT_KV_HEADS
ATT_Q = ATT_HEADS * ATT_DH
ATT_KV = ATT_KV_HEADS * ATT_DH
D_FF = 4 * D_MODEL
IN_W = 2 * RET_QK + 2 * RET_V + ATT_Q + 2 * ATT_KV
ROPE_THETA = 10000.0
EPS = 1e-6

OFF_RQ = 0
OFF_RK = OFF_RQ + RET_QK
OFF_RV = OFF_RK + RET_QK
OFF_RG = OFF_RV + RET_V
OFF_AQ = OFF_RG + RET_V
OFF_AK = OFF_AQ + ATT_Q
OFF_AV = OFF_AK + ATT_KV

PROJ_TN = 512
VMEM_LIMIT = 56 * 1024 * 1024

_NT = (((1,), (1,)), ((), ()))
_TN = (((0,), (0,)), ((), ()))


def _rope_halves(a, c, s):
    outs = []
    for h in range(a.shape[1] // 128):
        ah = a[:, h * 128:(h + 1) * 128]
        outs.append(ah * c + pltpu.roll(ah, 64, 1) * s)
    return outs[0] if len(outs) == 1 else jnp.concatenate(outs, axis=1)


def _head_norm(a, g):
    outs = []
    for h in range(a.shape[1] // 128):
        ah = a[:, h * 128:(h + 1) * 128]
        ms = jnp.mean(ah * ah, axis=-1, keepdims=True)
        outs.append(ah * lax.rsqrt(ms + EPS) * g)
    return outs[0] if len(outs) == 1 else jnp.concatenate(outs, axis=1)


def _in_proj_kernel(x_ref, ln_ref, w_ref, cr_ref, sr_ref, ca_ref, sa_ref, qg_ref, kg_ref,
                    o_ref, u_ref):
    j = pl.program_id(1)

    @pl.when(j == 0)
    def _():
        x = x_ref[...]
        ms = jnp.mean(x * x, axis=-1, keepdims=True)
        u_ref[...] = (x * lax.rsqrt(ms + EPS) * ln_ref[...]).astype(BF16)

    acc = jnp.dot(u_ref[...], w_ref[...], preferred_element_type=F32)

    @pl.when(j == OFF_RQ // PROJ_TN)
    def _():
        o_ref[...] = _rope_halves(acc, cr_ref[...], sr_ref[...]).astype(BF16)

    @pl.when(j == OFF_RK // PROJ_TN)
    def _():
        o_ref[...] = (_rope_halves(acc, cr_ref[...], sr_ref[...]) * (RET_DK ** -0.5)).astype(BF16)

    @pl.when((j >= OFF_RV // PROJ_TN) & (j < OFF_AQ // PROJ_TN))
    def _():
        o_ref[...] = acc.astype(BF16)

    @pl.when((j >= OFF_AQ // PROJ_TN) & (j < OFF_AK // PROJ_TN))
    def _():
        qn = _head_norm(acc, qg_ref[...])
        o_ref[...] = (_rope_halves(qn, ca_ref[...], sa_ref[...]) * (ATT_DH ** -0.5)).astype(BF16)

    @pl.when(j == OFF_AK // PROJ_TN)
    def _():
        kn = _head_norm(acc[:, :ATT_KV], kg_ref[...])
        o_ref[:, :ATT_KV] = _rope_halves(kn, ca_ref[...], sa_ref[...]).astype(BF16)
        o_ref[:, ATT_KV:] = acc[:, ATT_KV:].astype(BF16)


def _in_proj(x, ln1, w_in, cr, sr, ca, sa, qg, kg, *, tm, table_blocks):
    t = x.shape[0]
    n_j = IN_W // PROJ_TN
    tab = pl.BlockSpec((tm, 128), lambda i, j: (i % table_blocks, 0))
    vec = lambda n: pl.BlockSpec((1, n), lambda i, j: (0, 0))
    return pl.pallas_call(
        _in_proj_kernel,
        out_shape=jax.ShapeDtypeStruct((t, IN_W), BF16),
        grid=(t // tm, n_j),
        in_specs=[
            pl.BlockSpec((tm, D_MODEL), lambda i, j: (i, 0)),
            vec(D_MODEL),
            pl.BlockSpec((D_MODEL, PROJ_TN), lambda i, j: (0, j)),
            tab, tab, tab, tab,
            vec(128), vec(128),
        ],
        out_specs=pl.BlockSpec((tm, PROJ_TN), lambda i, j: (i, j)),
        scratch_shapes=[pltpu.VMEM((tm, D_MODEL), BF16)],
        compiler_params=pltpu.CompilerParams(
            dimension_semantics=("arbitrary", "arbitrary"), vmem_limit_bytes=VMEM_LIMIT),
        name="in_proj",
    )(x, ln1, w_in, cr, sr, ca, sa, qg, kg)


def _chunk_pos():
    return lax.broadcasted_iota(jnp.int32, (CHUNK, 1), 0).astype(F32)


def _ret_bwd_kernel(lgf_ref, lgb_ref, q_ref, k_ref, v_ref, part_ref, s_ref, d_ref):
    b = pl.program_id(0)
    i = pl.program_id(1)

    @pl.when((b == 0) & (i == 0))
    def _():
        r = lax.broadcasted_iota(jnp.int32, (CHUNK, CHUNK), 0)
        c = lax.broadcasted_iota(jnp.int32, (CHUNK, CHUNK), 1)
        diff = (r - c).astype(F32)
        for h in range(RET_HEADS):
            df = jnp.where(diff >= 0, jnp.exp(lgf_ref[h] * jnp.maximum(diff, 0.0)), 0.0)
            db = jnp.where(diff < 0, jnp.exp(lgb_ref[h] * jnp.maximum(-diff, 0.0)), 0.0)
            d_ref[h] = df + db

    @pl.when(i == 0)
    def _():
        s_ref[...] = jnp.zeros_like(s_ref)

    pos = _chunk_pos()
    for h in range(RET_HEADS):
        lg = lgb_ref[h]
        q = q_ref[:, h * RET_DK:(h + 1) * RET_DK]
        k = k_ref[:, h * RET_DK:(h + 1) * RET_DK]
        v = v_ref[:, h * RET_DV:(h + 1) * RET_DV]
        sc = lax.dot_general(q, k, _NT, preferred_element_type=F32) * d_ref[h]
        inner = jnp.dot(sc.astype(BF16), v, preferred_element_type=F32)
        q_dec = jnp.exp(lg * (float(CHUNK) - pos))
        cross = jnp.dot(q, s_ref[h].astype(BF16), preferred_element_type=F32) * q_dec
        part_ref[:, h * RET_DV:(h + 1) * RET_DV] = inner + cross
        k_dec = jnp.exp(lg * pos)
        kd = (k.astype(F32) * k_dec).astype(BF16)
        kv = lax.dot_general(kd, v, _TN, preferred_element_type=F32)
        c_dec = jnp.exp(lg * jnp.full((CHUNK, 1), float(CHUNK), F32))
        s_ref[h] = s_ref[h] * c_dec + kv


def _ret_fwd_kernel(lgf_ref, q_ref, k_ref, v_ref, g_ref, km_ref, vm_ref, part_ref,
                    o_ref, s_ref):
    i = pl.program_id(1)
    pos = _chunk_pos()

    @pl.when(i == 0)
    def _():
        for h in range(RET_HEADS):
            k_dec = jnp.exp(lgf_ref[h] * (float(CHUNK - 1) - pos))
            kd = (km_ref[:, h * RET_DK:(h + 1) * RET_DK].astype(F32) * k_dec).astype(BF16)
            s_ref[h] = lax.dot_general(kd, vm_ref[:, h * RET_DV:(h + 1) * RET_DV], _TN,
                                       preferred_element_type=F32)

    for h in range(RET_HEADS):
        lg = lgf_ref[h]
        q = q_ref[:, h * RET_DK:(h + 1) * RET_DK]
        k = k_ref[:, h * RET_DK:(h + 1) * RET_DK]
        v = v_ref[:, h * RET_DV:(h + 1) * RET_DV]
        q_dec = jnp.exp(lg * (pos + 1.0))
        cross = jnp.dot(q, s_ref[h].astype(BF16), preferred_element_type=F32) * q_dec
        tot = part_ref[:, h * RET_DV:(h + 1) * RET_DV] + cross
        ms = jnp.mean(tot * tot, axis=-1, keepdims=True)
        g = g_ref[:, h * RET_DV:(h + 1) * RET_DV].astype(F32)
        gate = g / (1.0 + jnp.exp(-g))
        o_ref[:, h * RET_DV:(h + 1) * RET_DV] = (tot * lax.rsqrt(ms + EPS) * gate).astype(BF16)
        k_dec = jnp.exp(lg * (float(CHUNK - 1) - pos))
        kd = (k.astype(F32) * k_dec).astype(BF16)
        kv = lax.dot_general(kd, v, _TN, preferred_element_type=F32)
        c_dec = jnp.exp(lg * jnp.full((CHUNK, 1), float(CHUNK), F32))
        s_ref[h] = s_ref[h] * c_dec + kv


def _retention(proj, proj_meta, lgf, lgb, *, batch, seq):
    t = proj.shape[0]
    n = seq // CHUNK
    smem = pl.BlockSpec(memory_space=pltpu.SMEM)
    params = pltpu.CompilerParams(dimension_semantics=("arbitrary", "arbitrary"),
                                  vmem_limit_bytes=VMEM_LIMIT)

    def rows_rev(b, i):
        return b * n + (n - 1 - i)

    part = pl.pallas_call(
        _ret_bwd_kernel,
        out_shape=jax.ShapeDtypeStruct((t, RET_V), F32),
        grid=(batch, n),
        in_specs=[
            smem, smem,
            pl.BlockSpec((CHUNK, RET_QK), lambda b, i: (rows_rev(b, i), OFF_RQ // RET_QK)),
            pl.BlockSpec((CHUNK, RET_QK), lambda b, i: (rows_rev(b, i), OFF_RK // RET_QK)),
            pl.BlockSpec((CHUNK, RET_V), lambda b, i: (rows_rev(b, i), OFF_RV // RET_V)),
        ],
        out_specs=pl.BlockSpec((CHUNK, RET_V), lambda b, i: (rows_rev(b, i), 0)),
        scratch_shapes=[pltpu.VMEM((RET_HEADS, RET_DK, RET_DV), F32),
                        pltpu.VMEM((RET_HEADS, CHUNK, CHUNK), F32)],
        compiler_params=params,
        name="retention_bwd",
    )(lgf, lgb, proj, proj, proj)

    def rows(b, i):
        return b * n + i

    return pl.pallas_call(
        _ret_fwd_kernel,
        out_shape=jax.ShapeDtypeStruct((t, RET_V), BF16),
        grid=(batch, n),
        in_specs=[
            smem,
            pl.BlockSpec((CHUNK, RET_QK), lambda b, i: (rows(b, i), OFF_RQ // RET_QK)),
            pl.BlockSpec((CHUNK, RET_QK), lambda b, i: (rows(b, i), OFF_RK // RET_QK)),
            pl.BlockSpec((CHUNK, RET_V), lambda b, i: (rows(b, i), OFF_RV // RET_V)),
            pl.BlockSpec((CHUNK, RET_V), lambda b, i: (rows(b, i), OFF_RG // RET_V)),
            pl.BlockSpec((CHUNK, RET_QK), lambda b, i: (0, OFF_RK // RET_QK)),
            pl.BlockSpec((CHUNK, RET_V), lambda b, i: (0, OFF_RV // RET_V)),
            pl.BlockSpec((CHUNK, RET_V), lambda b, i: (rows(b, i), 0)),
        ],
        out_specs=pl.BlockSpec((CHUNK, RET_V), lambda b, i: (rows(b, i), 0)),
        scratch_shapes=[pltpu.VMEM((RET_HEADS, RET_DK, RET_DV), F32)],
        compiler_params=params,
        name="retention_fwd",
    )(lgf, proj, proj, proj, proj, proj_meta, proj_meta, part)


def _attn_kernel(q_ref, k_ref, v_ref, km_ref, vm_ref, o_ref, m_ref, l_ref, acc_ref):
    ki = pl.program_id(3)

    @pl.when(ki == 0)
    def _():
        km = km_ref[...]
        vm = vm_ref[...]
        for g in range(ATT_GROUP):
            q = q_ref[:, g * ATT_DH:(g + 1) * ATT_DH]
            s = lax.dot_general(q, km, _NT, preferred_element_type=F32)
            m = jnp.max(s, axis=-1, keepdims=True)
            p = jnp.exp(s - m)
            m_ref[g] = m
            l_ref[g] = jnp.sum(p, axis=-1, keepdims=True)
            acc_ref[g] = jnp.dot(p.astype(BF16), vm, preferred_element_type=F32)

    k = k_ref[...]
    v = v_ref[...]
    for g in range(ATT_GROUP):
        q = q_ref[:, g * ATT_DH:(g + 1) * ATT_DH]
        s = lax.dot_general(q, k, _NT, preferred_element_type=F32)
        m_prev = m_ref[g]
        m_new = jnp.maximum(m_prev, jnp.max(s, axis=-1, keepdims=True))
        alpha = jnp.exp(m_prev - m_new)
        p = jnp.exp(s - m_new)
        l_ref[g] = alpha * l_ref[g] + jnp.sum(p, axis=-1, keepdims=True)
        acc_ref[g] = alpha * acc_ref[g] + jnp.dot(p.astype(BF16), v, preferred_element_type=F32)
        m_ref[g] = m_new

    @pl.when(ki == pl.num_programs(3) - 1)
    def _():
        for g in range(ATT_GROUP):
            o_ref[:, g * ATT_DH:(g + 1) * ATT_DH] = (acc_ref[g] / l_ref[g]).astype(BF16)


def _attention(proj, k_meta, v_meta, *, batch, seq, tq, tk):
    t = proj.shape[0]
    nq = seq // tq
    nk = seq // tk
    qw = ATT_GROUP * ATT_DH
    return pl.pallas_call(
        _attn_kernel,
        out_shape=jax.ShapeDtypeStruct((t, ATT_Q), BF16),
        grid=(batch, ATT_KV_HEADS, nq, nk),
        in_specs=[
            pl.BlockSpec((tq, qw), lambda b, h, qi, ki: (b * nq + qi, OFF_AQ // qw + h)),
            pl.BlockSpec((tk, ATT_DH), lambda b, h, qi, ki: (b * nk + ki, OFF_AK // ATT_DH + h)),
            pl.BlockSpec((tk, ATT_DH), lambda b, h, qi, ki: (b * nk + ki, OFF_AV // ATT_DH + h)),
            pl.BlockSpec((None, N_META, ATT_DH), lambda b, h, qi, ki: (h, 0, 0)),
            pl.BlockSpec((None, N_META, ATT_DH), lambda b, h, qi, ki: (h, 0, 0)),
        ],
        out_specs=pl.BlockSpec((tq, qw), lambda b, h, qi, ki: (b * nq + qi, h)),
        scratch_shapes=[pltpu.VMEM((ATT_GROUP, tq, 1), F32),
                        pltpu.VMEM((ATT_GROUP, tq, 1), F32),
                        pltpu.VMEM((ATT_GROUP, tq, ATT_DH), F32)],
        compiler_params=pltpu.CompilerParams(
            dimension_semantics=("arbitrary",) * 4, vmem_limit_bytes=VMEM_LIMIT),
        name="attention",
    )(proj, proj, proj, k_meta, v_meta)


def _out_proj_kernel(x_ref, r_ref, a_ref, wr_ref, wa_ref, ln_ref, h_ref, u_ref):
    h = (x_ref[...]
         + jnp.dot(r_ref[...], wr_ref[...], preferred_element_type=F32)
         + jnp.dot(a_ref[...], wa_ref[...], preferred_element_type=F32))
    h_ref[...] = h
    ms = jnp.mean(h * h, axis=-1, keepdims=True)
    u_ref[...] = (h * lax.rsqrt(ms + EPS) * ln_ref[...]).astype(BF16)


def _out_proj(x, ret, att, w_out, ln2, *, tm):
    t = x.shape[0]
    row = lambda w: pl.BlockSpec((tm, w), lambda i: (i, 0))
    return pl.pallas_call(
        _out_proj_kernel,
        out_shape=(jax.ShapeDtypeStruct((t, D_MODEL), F32),
                   jax.ShapeDtypeStruct((t, D_MODEL), BF16)),
        grid=(t // tm,),
        in_specs=[
            row(D_MODEL), row(RET_V), row(ATT_Q),
            pl.BlockSpec((RET_V, D_MODEL), lambda i: (0, 0)),
            pl.BlockSpec((ATT_Q, D_MODEL), lambda i: (1, 0)),
            pl.BlockSpec((1, D_MODEL), lambda i: (0, 0)),
        ],
        out_specs=(row(D_MODEL), row(D_MODEL)),
        compiler_params=pltpu.CompilerParams(
            dimension_semantics=("arbitrary",), vmem_limit_bytes=VMEM_LIMIT),
        name="out_proj",
    )(x, ret, att, w_out, w_out, ln2)


def _mlp_kernel(u_ref, h_ref, wu_ref, wd_ref, fg_ref, o_ref):
    j = pl.program_id(1)

    @pl.when(j == 0)
    def _():
        o_ref[...] = h_ref[...]

    a = jnp.dot(u_ref[...], wu_ref[...], preferred_element_type=F32)
    a = jnp.square(jnp.maximum(a, 0.0)).astype(BF16)
    o_ref[...] += jnp.dot(a, wd_ref[...], preferred_element_type=F32)

    @pl.when(j == pl.num_programs(1) - 1)
    def _():
        h = o_ref[...]
        ms = jnp.mean(h * h, axis=-1, keepdims=True)
        o_ref[...] = h * lax.rsqrt(ms + EPS) * fg_ref[...]


def _mlp(u, h, w_up, w_down, fg, *, tm, tf):
    t = u.shape[0]
    return pl.pallas_call(
        _mlp_kernel,
        out_shape=jax.ShapeDtypeStruct((t, D_MODEL), F32),
        grid=(t // tm, D_FF // tf),
        in_specs=[
            pl.BlockSpec((tm, D_MODEL), lambda i, j: (i, 0)),
            pl.BlockSpec((tm, D_MODEL), lambda i, j: (i, 0)),
            pl.BlockSpec((D_MODEL, tf), lambda i, j: (0, j)),
            pl.BlockSpec((tf, D_MODEL), lambda i, j: (j, 0)),
            pl.BlockSpec((1, D_MODEL), lambda i, j: (0, 0)),
        ],
        out_specs=pl.BlockSpec((tm, D_MODEL), lambda i, j: (i, 0)),
        compiler_params=pltpu.CompilerParams(
            dimension_semantics=("arbitrary", "arbitrary"), vmem_limit_bytes=VMEM_LIMIT),
        name="mlp",
    )(u, h, w_up, w_down, fg)


def _deinterleave_cols(w, n_heads):
    d = w.shape[0]
    return w.reshape(d, n_heads, 64, 2).transpose(0, 1, 3, 2).reshape(d, n_heads * 128)


def _rope_tables(seq):
    pos = jnp.arange(CHUNK + seq, dtype=F32)
    freq_r = ROPE_THETA ** (-jnp.linspace(0.0, 1.0, RET_DK // 2, dtype=F32))
    ang_r = pos[:, None] * freq_r[None]
    cr = jnp.concatenate([jnp.cos(ang_r)] * 2, axis=-1)
    sr = jnp.concatenate([-jnp.sin(ang_r), jnp.sin(ang_r)], axis=-1)
    tok = jnp.arange(seq)
    row = (tok // GRID_W).astype(F32)
    col = (tok % GRID_W).astype(F32)
    n_pair = ATT_DH // 4
    freq_a = ROPE_THETA ** (-jnp.arange(n_pair, dtype=F32) / n_pair)
    ang_a = jnp.concatenate([row[:, None] * freq_a[None], col[:, None] * freq_a[None]], axis=-1)
    ca = jnp.concatenate([jnp.cos(ang_a)] * 2, axis=-1)
    sa = jnp.concatenate([-jnp.sin(ang_a), jnp.sin(ang_a)], axis=-1)
    return cr, sr, ca, sa


def _encode(x, wts, tabs, meta, *, attn_tq, attn_tk):
    batch, seq, _ = x.shape
    cr, sr, ca, sa = tabs
    proj_meta, k_meta, v_meta = meta
    x2 = x.reshape(batch * seq, D_MODEL)
    tm = 1024
    proj = _in_proj(x2, wts["ln1"], wts["w_in"], cr, sr, ca, sa, wts["qg"], wts["kg"],
                    tm=tm, table_blocks=seq // tm)
    ret = _retention(proj, proj_meta, wts["lgf"], wts["lgb"], batch=batch, seq=seq)
    att = _attention(proj, k_meta, v_meta, batch=batch, seq=seq, tq=attn_tq, tk=attn_tk)
    h1, u2 = _out_proj(x2, ret, att, wts["w_out"], wts["ln2"], tm=512)
    y = _mlp(u2, h1, wts["w_up"], wts["w_down"], wts["fg"], tm=512, tf=1024)
    return y.reshape(batch, seq, D_MODEL)


def kernel(x_prompt, x_sample, meta_tokens, ln1_g, w_in, q_norm_g, k_norm_g, ret_log_decay_fwd,
           ret_log_decay_bwd, w_out, ln2_g, w_up, w_down, final_norm_g):
    assert w_in.shape[0] == 1, "meta-token residual stream is only skippable for a single layer"
    wi = w_in[0]
    w_in_p = jnp.concatenate([
        _deinterleave_cols(wi[:, OFF_RQ:OFF_RV], 2 * RET_HEADS),
        wi[:, OFF_RV:OFF_AQ],
        _deinterleave_cols(wi[:, OFF_AQ:OFF_AV], ATT_HEADS + ATT_KV_HEADS),
        wi[:, OFF_AV:],
    ], axis=1).astype(BF16)
    deint_vec = lambda g: g.reshape(64, 2).T.reshape(1, 128)
    wts = {
        "ln1": ln1_g[0].reshape(1, D_MODEL),
        "w_in": w_in_p,
        "qg": deint_vec(q_norm_g[0]),
        "kg": deint_vec(k_norm_g[0]),
        "lgf": ret_log_decay_fwd[0],
        "lgb": ret_log_decay_bwd[0],
        "w_out": w_out[0].astype(BF16),
        "ln2": ln2_g[0].reshape(1, D_MODEL),
        "w_up": w_up[0].astype(BF16),
        "w_down": w_down[0].astype(BF16),
        "fg": final_norm_g.reshape(1, D_MODEL),
    }
    max_seq = max(x_prompt.shape[1], x_sample.shape[1])
    cr, sr, ca, sa = _rope_tables(max_seq)

    x_meta = jnp.concatenate(
        [jnp.zeros((CHUNK - N_META, D_MODEL), F32), meta_tokens.astype(F32)], axis=0)
    ones = jnp.ones((CHUNK, 128), F32)
    proj_meta = _in_proj(x_meta, wts["ln1"], w_in_p, cr[:CHUNK], sr[:CHUNK], ones, 0.0 * ones,
                         wts["qg"], wts["kg"], tm=CHUNK, table_blocks=1)
    k_meta = proj_meta[CHUNK - N_META:, OFF_AK:OFF_AV].reshape(N_META, ATT_KV_HEADS, ATT_DH)
    v_meta = proj_meta[CHUNK - N_META:, OFF_AV:].reshape(N_META, ATT_KV_HEADS, ATT_DH)
    meta = (proj_meta, k_meta.transpose(1, 0, 2), v_meta.transpose(1, 0, 2))

    tabs = (cr[CHUNK:], sr[CHUNK:], ca, sa)
    y_prompt = _encode(x_prompt, wts, tabs, meta, attn_tq=512, attn_tk=1024)
    y_sample = _encode(x_sample, wts, tabs, meta, attn_tq=512, attn_tk=1024)
    return (y_prompt, y_sample)
```

```python
import jax
import jax.numpy as jnp
from jax import lax
from jax.experimental import pallas as pl
from jax.experimental.pallas import tpu as pltpu

F32 = jnp.float32
BF16 = jnp.bfloat16

D_MODEL = 2048
N_META = 16
GRID_W = 64
CHUNK = 128
RET_HEADS = 4
RET_DK = 128
RET_DV = 256
RET_QK = RET_HEADS * RET_DK
RET_V = RET_HEADS * RET_DV
ATT_DH = 128
ATT_HEADS = 8
ATT_KV_HEADS = 2
ATT_GROUP = ATT_HEADS // ATT_KV_HEADS
ATT_Q = ATT_HEADS * ATT_DH
ATT_KV = ATT_KV_HEADS * ATT_DH
D_FF = 4 * D_MODEL
IN_W = 2 * RET_QK + 2 * RET_V + ATT_Q + 2 * ATT_KV
ROPE_THETA = 10000.0
EPS = 1e-6
LOG2E = 1.4426950408889634

OFF_RQ = 0
OFF_RK = OFF_RQ + RET_QK
OFF_RV = OFF_RK + RET_QK
OFF_RG = OFF_RV + RET_V
OFF_AQ = OFF_RG + RET_V
OFF_AK = OFF_AQ + ATT_Q
OFF_AV = OFF_AK + ATT_KV

PROJ_TN = 512
N_RET_BLOCKS = OFF_AQ // PROJ_TN
VMEM_LIMIT = 56 * 1024 * 1024

_NT = (((1,), (1,)), ((), ()))
_TN = (((0,), (0,)), ((), ()))


def _rope_halves(a, c, s):
    outs = []
    for h in range(a.shape[1] // 128):
        ah = a[:, h * 128:(h + 1) * 128]
        outs.append(ah * c + pltpu.roll(ah, 64, 1) * s)
    return outs[0] if len(outs) == 1 else jnp.concatenate(outs, axis=1)


def _head_norm(a, g):
    outs = []
    for h in range(a.shape[1] // 128):
        ah = a[:, h * 128:(h + 1) * 128]
        ms = jnp.mean(ah * ah, axis=-1, keepdims=True)
        outs.append(ah * lax.rsqrt(ms + EPS) * g)
    return outs[0] if len(outs) == 1 else jnp.concatenate(outs, axis=1)


def _in_proj_kernel(x_ref, ln_ref, w_ref, cr_ref, sr_ref, ca_ref, sa_ref, qg_ref, kg_ref,
                    o_ref, qt_ref, ak_ref, vt_ref, u_ref):
    j = pl.program_id(1)

    @pl.when(j == 0)
    def _():
        x = x_ref[...]
        ms = jnp.mean(x * x, axis=-1, keepdims=True)
        u_ref[...] = (x * lax.rsqrt(ms + EPS) * ln_ref[...]).astype(BF16)

    acc = jnp.dot(u_ref[...], w_ref[...], preferred_element_type=F32)

    @pl.when(j == OFF_RQ // PROJ_TN)
    def _():
        o_ref[...] = _rope_halves(acc, cr_ref[...], sr_ref[...]).astype(BF16)

    @pl.when(j == OFF_RK // PROJ_TN)
    def _():
        o_ref[...] = (_rope_halves(acc, cr_ref[...], sr_ref[...]) * (RET_DK ** -0.5)).astype(BF16)

    @pl.when((j >= OFF_RV // PROJ_TN) & (j < N_RET_BLOCKS))
    def _():
        o_ref[...] = acc.astype(BF16)

    @pl.when((j >= OFF_AQ // PROJ_TN) & (j < OFF_AK // PROJ_TN))
    def _():
        qn = _head_norm(acc, qg_ref[...])
        q = _rope_halves(qn, ca_ref[...], sa_ref[...]) * (ATT_DH ** -0.5 * LOG2E)
        qt_ref[...] = q.T.astype(BF16)

    @pl.when(j == OFF_AK // PROJ_TN)
    def _():
        kn = _head_norm(acc[:, :ATT_KV], kg_ref[...])
        ak_ref[...] = _rope_halves(kn, ca_ref[...], sa_ref[...]).astype(BF16)
        vt_ref[...] = acc[:, ATT_KV:].T.astype(BF16)


def _in_proj(x, ln1, w_in, cr, sr, ca, sa, qg, kg, *, tm, table_blocks):
    t = x.shape[0]
    n_j = IN_W // PROJ_TN
    j_aq = OFF_AQ // PROJ_TN
    tab = pl.BlockSpec((tm, 128), lambda i, j: (i % table_blocks, 0))
    vec = lambda n: pl.BlockSpec((1, n), lambda i, j: (0, 0))
    return pl.pallas_call(
        _in_proj_kernel,
        out_shape=(jax.ShapeDtypeStruct((t, OFF_AQ), BF16),
                   jax.ShapeDtypeStruct((ATT_Q, t), BF16),
                   jax.ShapeDtypeStruct((t, ATT_KV), BF16),
                   jax.ShapeDtypeStruct((ATT_KV, t), BF16)),
        grid=(t // tm, n_j),
        in_specs=[
            pl.BlockSpec((tm, D_MODEL), lambda i, j: (i, 0)),
            vec(D_MODEL),
            pl.BlockSpec((D_MODEL, PROJ_TN), lambda i, j: (0, j)),
            tab, tab, tab, tab,
            vec(128), vec(128),
        ],
        out_specs=(
            pl.BlockSpec((tm, PROJ_TN), lambda i, j: (i, jnp.minimum(j, N_RET_BLOCKS - 1))),
            pl.BlockSpec((PROJ_TN, tm), lambda i, j: (jnp.clip(j - j_aq, 0, 1), i)),
            pl.BlockSpec((tm, ATT_KV), lambda i, j: (i, 0)),
            pl.BlockSpec((ATT_KV, tm), lambda i, j: (0, i)),
        ),
        scratch_shapes=[pltpu.VMEM((tm, D_MODEL), BF16)],
        compiler_params=pltpu.CompilerParams(
            dimension_semantics=("arbitrary", "arbitrary"), vmem_limit_bytes=VMEM_LIMIT),
        name="in_proj",
    )(x, ln1, w_in, cr, sr, ca, sa, qg, kg)


def _chunk_pos():
    return lax.broadcasted_iota(jnp.int32, (CHUNK, 1), 0).astype(F32)


def _ret_bwd_kernel(lgf_ref, lgb_ref, q_ref, k_ref, v_ref, part_ref, s_ref, d_ref):
    b = pl.program_id(0)
    i = pl.program_id(1)

    @pl.when((b == 0) & (i == 0))
    def _():
        r = lax.broadcasted_iota(jnp.int32, (CHUNK, CHUNK), 0)
        c = lax.broadcasted_iota(jnp.int32, (CHUNK, CHUNK), 1)
        diff = (r - c).astype(F32)
        for h in range(RET_HEADS):
            df = jnp.where(diff >= 0, jnp.exp(lgf_ref[h] * jnp.maximum(diff, 0.0)), 0.0)
            db = jnp.where(diff < 0, jnp.exp(lgb_ref[h] * jnp.maximum(-diff, 0.0)), 0.0)
            d_ref[h] = df + db

    @pl.when(i == 0)
    def _():
        s_ref[...] = jnp.zeros_like(s_ref)

    pos = _chunk_pos()
    for h in range(RET_HEADS):
        lg = lgb_ref[h]
        q = q_ref[:, h * RET_DK:(h + 1) * RET_DK]
        k = k_ref[:, h * RET_DK:(h + 1) * RET_DK]
        v = v_ref[:, h * RET_DV:(h + 1) * RET_DV]
        sc = lax.dot_general(q, k, _NT, preferred_element_type=F32) * d_ref[h]
        inner = jnp.dot(sc.astype(BF16), v, preferred_element_type=F32)
        q_dec = jnp.exp(lg * (float(CHUNK) - pos))
        cross = jnp.dot(q, s_ref[h].astype(BF16), preferred_element_type=F32) * q_dec
        part_ref[:, h * RET_DV:(h + 1) * RET_DV] = inner + cross
        k_dec = jnp.exp(lg * pos)
        kd = (k.astype(F32) * k_dec).astype(BF16)
        kv = lax.dot_general(kd, v, _TN, preferred_element_type=F32)
        c_dec = jnp.exp(lg * jnp.full((CHUNK, 1), float(CHUNK), F32))
        s_ref[h] = s_ref[h] * c_dec + kv


def _ret_fwd_kernel(lgf_ref, q_ref, k_ref, v_ref, g_ref, km_ref, vm_ref, part_ref,
                    o_ref, s_ref):
    i = pl.program_id(1)
    pos = _chunk_pos()

    @pl.when(i == 0)
    def _():
        for h in range(RET_HEADS):
            k_dec = jnp.exp(lgf_ref[h] * (float(CHUNK - 1) - pos))
            kd = (km_ref[:, h * RET_DK:(h + 1) * RET_DK].astype(F32) * k_dec).astype(BF16)
            s_ref[h] = lax.dot_general(kd, vm_ref[:, h * RET_DV:(h + 1) * RET_DV], _TN,
                                       preferred_element_type=F32)

    for h in range(RET_HEADS):
        lg = lgf_ref[h]
        q = q_ref[:, h * RET_DK:(h + 1) * RET_DK]
        k = k_ref[:, h * RET_DK:(h + 1) * RET_DK]
        v = v_ref[:, h * RET_DV:(h + 1) * RET_DV]
        q_dec = jnp.exp(lg * (pos + 1.0))
        cross = jnp.dot(q, s_ref[h].astype(BF16), preferred_element_type=F32) * q_dec
        tot = part_ref[:, h * RET_DV:(h + 1) * RET_DV] + cross
        ms = jnp.mean(tot * tot, axis=-1, keepdims=True)
        g = g_ref[:, h * RET_DV:(h + 1) * RET_DV].astype(F32)
        gate = g / (1.0 + jnp.exp(-g))
        o_ref[:, h * RET_DV:(h + 1) * RET_DV] = (tot * lax.rsqrt(ms + EPS) * gate).astype(BF16)
        k_dec = jnp.exp(lg * (float(CHUNK - 1) - pos))
        kd = (k.astype(F32) * k_dec).astype(BF16)
        kv = lax.dot_general(kd, v, _TN, preferred_element_type=F32)
        c_dec = jnp.exp(lg * jnp.full((CHUNK, 1), float(CHUNK), F32))
        s_ref[h] = s_ref[h] * c_dec + kv


def _retention(proj, proj_meta, lgf, lgb, *, batch, seq):
    t = proj.shape[0]
    n = seq // CHUNK
    smem = pl.BlockSpec(memory_space=pltpu.SMEM)
    params = pltpu.CompilerParams(dimension_semantics=("arbitrary", "arbitrary"),
                                  vmem_limit_bytes=VMEM_LIMIT)

    def rows_rev(b, i):
        return b * n + (n - 1 - i)

    part = pl.pallas_call(
        _ret_bwd_kernel,
        out_shape=jax.ShapeDtypeStruct((t, RET_V), F32),
        grid=(batch, n),
        in_specs=[
            smem, smem,
            pl.BlockSpec((CHUNK, RET_QK), lambda b, i: (rows_rev(b, i), OFF_RQ // RET_QK)),
            pl.BlockSpec((CHUNK, RET_QK), lambda b, i: (rows_rev(b, i), OFF_RK // RET_QK)),
            pl.BlockSpec((CHUNK, RET_V), lambda b, i: (rows_rev(b, i), OFF_RV // RET_V)),
        ],
        out_specs=pl.BlockSpec((CHUNK, RET_V), lambda b, i: (rows_rev(b, i), 0)),
        scratch_shapes=[pltpu.VMEM((RET_HEADS, RET_DK, RET_DV), F32),
                        pltpu.VMEM((RET_HEADS, CHUNK, CHUNK), F32)],
        compiler_params=params,
        name="retention_bwd",
    )(lgf, lgb, proj, proj, proj)

    def rows(b, i):
        return b * n + i

    return pl.pallas_call(
        _ret_fwd_kernel,
        out_shape=jax.ShapeDtypeStruct((t, RET_V), BF16),
        grid=(batch, n),
        in_specs=[
            smem,
            pl.BlockSpec((CHUNK, RET_QK), lambda b, i: (rows(b, i), OFF_RQ // RET_QK)),
            pl.BlockSpec((CHUNK, RET_QK), lambda b, i: (rows(b, i), OFF_RK // RET_QK)),
            pl.BlockSpec((CHUNK, RET_V), lambda b, i: (rows(b, i), OFF_RV // RET_V)),
            pl.BlockSpec((CHUNK, RET_V), lambda b, i: (rows(b, i), OFF_RG // RET_V)),
            pl.BlockSpec((CHUNK, RET_QK), lambda b, i: (0, OFF_RK // RET_QK)),
            pl.BlockSpec((CHUNK, RET_V), lambda b, i: (0, OFF_RV // RET_V)),
            pl.BlockSpec((CHUNK, RET_V), lambda b, i: (rows(b, i), 0)),
        ],
        out_specs=pl.BlockSpec((CHUNK, RET_V), lambda b, i: (rows(b, i), 0)),
        scratch_shapes=[pltpu.VMEM((RET_HEADS, RET_DK, RET_DV), F32)],
        compiler_params=params,
        name="retention_fwd",
    )(lgf, proj, proj, proj, proj, proj_meta, proj_meta, part)


ATT_SUB = 256
ATT_BOUND_MAX = 60.0


def _attn_subtile(k, vt, mask, qt_ref, m_ref, l_ref, acc_ref, g, c):
    qs = slice(c * ATT_SUB, (c + 1) * ATT_SUB)
    st = jnp.dot(k, qt_ref[g * ATT_DH:(g + 1) * ATT_DH, qs], preferred_element_type=F32)
    if mask is not None:
        st = jnp.where(mask, st, -jnp.inf)
    m_prev = m_ref[g, :, qs]
    m_new = jnp.maximum(m_prev, jnp.max(st, axis=0, keepdims=True))
    alpha = jnp.exp2(m_prev - m_new)
    p = jnp.exp2(st - m_new)
    l_ref[g, :, qs] = alpha * l_ref[g, :, qs] + jnp.sum(p, axis=0, keepdims=True)
    acc_ref[g, :, qs] = alpha * acc_ref[g, :, qs] + jnp.dot(
        vt, p.astype(BF16), preferred_element_type=F32)
    m_ref[g, :, qs] = m_new


def _attn_kernel(qt_ref, k_ref, vt_ref, km_ref, vtm_ref, o_ref, m_ref, l_ref, acc_ref):
    ki = pl.program_id(3)
    tq = qt_ref.shape[1]
    tk = k_ref.shape[0]

    @pl.when(ki == 0)
    def _():
        m_ref[...] = jnp.full_like(m_ref, -jnp.inf)
        l_ref[...] = jnp.zeros_like(l_ref)
        acc_ref[...] = jnp.zeros_like(acc_ref)
        is_meta = lax.broadcasted_iota(jnp.int32, (CHUNK, ATT_SUB), 0) >= CHUNK - N_META
        for g in range(ATT_GROUP):
            for c in range(tq // ATT_SUB):
                _attn_subtile(km_ref[...], vtm_ref[...], is_meta, qt_ref, m_ref, l_ref, acc_ref,
                              g, c)

    for r in range(tk // ATT_SUB):
        ks = slice(r * ATT_SUB, (r + 1) * ATT_SUB)
        for g in range(ATT_GROUP):
            for c in range(tq // ATT_SUB):
                _attn_subtile(k_ref[ks, :], vt_ref[:, ks], None, qt_ref, m_ref, l_ref, acc_ref,
                              g, c)

    @pl.when(ki == pl.num_programs(3) - 1)
    def _():
        for g in range(ATT_GROUP):
            o = acc_ref[g] / l_ref[g]
            o_ref[:, g * ATT_DH:(g + 1) * ATT_DH] = o.T.astype(BF16)


def _attn_bounded_kernel(bound_ref, qt_ref, k_ref, vt_ref, km_ref, vtm_ref, o_ref, l_ref, acc_ref):
    ki = pl.program_id(3)
    tq = qt_ref.shape[1]
    bound = bound_ref[0]

    def update(k, vt, mask, g):
        subs = [slice(c * ATT_SUB, (c + 1) * ATT_SUB) for c in range(tq // ATT_SUB)]
        sts = [jnp.dot(k, qt_ref[g * ATT_DH:(g + 1) * ATT_DH, qs], preferred_element_type=F32)
               for qs in subs]
        ps = [jnp.exp2(st - bound) for st in sts]
        if mask is not None:
            ps = [jnp.where(mask, p, 0.0) for p in ps]
        for qs, p in zip(subs, ps):
            l_ref[g, :, qs] += p.reshape(-1, 8, ATT_SUB).sum(axis=0)
        for qs, p in zip(subs, ps):
            acc_ref[g, :, qs] += jnp.dot(vt, p.astype(BF16), preferred_element_type=F32)

    @pl.when(ki == 0)
    def _():
        l_ref[...] = jnp.zeros_like(l_ref)
        acc_ref[...] = jnp.zeros_like(acc_ref)
        is_meta = lax.broadcasted_iota(jnp.int32, (CHUNK, ATT_SUB), 0) >= CHUNK - N_META
        for g in range(ATT_GROUP):
            update(km_ref[...], vtm_ref[...], is_meta, g)

    for g in range(ATT_GROUP):
        update(k_ref[...], vt_ref[...], None, g)

    @pl.when(ki == pl.num_programs(3) - 1)
    def _():
        for g in range(ATT_GROUP):
            o = acc_ref[g] / jnp.sum(l_ref[g], axis=0, keepdims=True)
            o_ref[:, g * ATT_DH:(g + 1) * ATT_DH] = o.T.astype(BF16)


def _attention(qt, ak, vt, ak_meta, vt_meta, bound, *, batch, seq, tq, tk, bounded):
    t = ak.shape[0]
    nq = seq // tq
    nk = seq // tk
    qw = ATT_GROUP * ATT_DH
    in_specs = [
        pl.BlockSpec((qw, tq), lambda b, h, qi, ki: (h, b * nq + qi)),
        pl.BlockSpec((tk, ATT_DH), lambda b, h, qi, ki: (b * nk + ki, h)),
        pl.BlockSpec((ATT_DH, tk), lambda b, h, qi, ki: (h, b * nk + ki)),
        pl.BlockSpec((CHUNK, ATT_DH), lambda b, h, qi, ki: (0, h)),
        pl.BlockSpec((ATT_DH, CHUNK), lambda b, h, qi, ki: (h, 0)),
    ]
    acc = pltpu.VMEM((ATT_GROUP, ATT_DH, tq), F32)
    if bounded:
        body, args = _attn_bounded_kernel, (bound, qt, ak, vt, ak_meta, vt_meta)
        in_specs = [pl.BlockSpec(memory_space=pltpu.SMEM)] + in_specs
        scratch = [pltpu.VMEM((ATT_GROUP, 8, tq), F32), acc]
    else:
        body, args = _attn_kernel, (qt, ak, vt, ak_meta, vt_meta)
        scratch = [pltpu.VMEM((ATT_GROUP, 1, tq), F32), pltpu.VMEM((ATT_GROUP, 1, tq), F32), acc]
    return pl.pallas_call(
        body,
        out_shape=jax.ShapeDtypeStruct((t, ATT_Q), BF16),
        grid=(batch, ATT_KV_HEADS, nq, nk),
        in_specs=in_specs,
        out_specs=pl.BlockSpec((tq, qw), lambda b, h, qi, ki: (b * nq + qi, h)),
        scratch_shapes=scratch,
        compiler_params=pltpu.CompilerParams(
            dimension_semantics=("arbitrary",) * 4, vmem_limit_bytes=VMEM_LIMIT),
        name="attention_bounded" if bounded else "attention",
    )(*args)


def _out_proj_kernel(x_ref, r_ref, a_ref, wr_ref, wa_ref, ln_ref, h_ref, u_ref):
    h = (x_ref[...]
         + jnp.dot(r_ref[...], wr_ref[...], preferred_element_type=F32)
         + jnp.dot(a_ref[...], wa_ref[...], preferred_element_type=F32))
    h_ref[...] = h
    ms = jnp.mean(h * h, axis=-1, keepdims=True)
    u_ref[...] = (h * lax.rsqrt(ms + EPS) * ln_ref[...]).astype(BF16)


def _out_proj(x, ret, att, w_out, ln2, *, tm):
    t = x.shape[0]
    row = lambda w: pl.BlockSpec((tm, w), lambda i: (i, 0))
    return pl.pallas_call(
        _out_proj_kernel,
        out_shape=(jax.ShapeDtypeStruct((t, D_MODEL), F32),
                   jax.ShapeDtypeStruct((t, D_MODEL), BF16)),
        grid=(t // tm,),
        in_specs=[
            row(D_MODEL), row(RET_V), row(ATT_Q),
            pl.BlockSpec((RET_V, D_MODEL), lambda i: (0, 0)),
            pl.BlockSpec((ATT_Q, D_MODEL), lambda i: (1, 0)),
            pl.BlockSpec((1, D_MODEL), lambda i: (0, 0)),
        ],
        out_specs=(row(D_MODEL), row(D_MODEL)),
        compiler_params=pltpu.CompilerParams(
            dimension_semantics=("arbitrary",), vmem_limit_bytes=VMEM_LIMIT),
        name="out_proj",
    )(x, ret, att, w_out, w_out, ln2)


def _mlp_kernel(u_ref, h_ref, wu_ref, wd_ref, fg_ref, o_ref):
    j = pl.program_id(1)

    @pl.when(j == 0)
    def _():
        o_ref[...] = h_ref[...]

    a = jnp.dot(u_ref[...], wu_ref[...], preferred_element_type=F32)
    a = jnp.square(jnp.maximum(a, 0.0)).astype(BF16)
    o_ref[...] += jnp.dot(a, wd_ref[...], preferred_element_type=F32)

    @pl.when(j == pl.num_programs(1) - 1)
    def _():
        h = o_ref[...]
        ms = jnp.mean(h * h, axis=-1, keepdims=True)
        o_ref[...] = h * lax.rsqrt(ms + EPS) * fg_ref[...]


def _mlp(u, h, w_up, w_down, fg, *, tm, tf):
    t = u.shape[0]
    return pl.pallas_call(
        _mlp_kernel,
        out_shape=jax.ShapeDtypeStruct((t, D_MODEL), F32),
        grid=(t // tm, D_FF // tf),
        in_specs=[
            pl.BlockSpec((tm, D_MODEL), lambda i, j: (i, 0)),
            pl.BlockSpec((tm, D_MODEL), lambda i, j: (i, 0)),
            pl.BlockSpec((D_MODEL, tf), lambda i, j: (0, j)),
            pl.BlockSpec((tf, D_MODEL), lambda i, j: (j, 0)),
            pl.BlockSpec((1, D_MODEL), lambda i, j: (0, 0)),
        ],
        out_specs=pl.BlockSpec((tm, D_MODEL), lambda i, j: (i, 0)),
        compiler_params=pltpu.CompilerParams(
            dimension_semantics=("arbitrary", "arbitrary"), vmem_limit_bytes=VMEM_LIMIT),
        name="mlp",
    )(u, h, w_up, w_down, fg)


def _deinterleave_cols(w, n_heads):
    d = w.shape[0]
    return w.reshape(d, n_heads, 64, 2).transpose(0, 1, 3, 2).reshape(d, n_heads * 128)


def _rope_tables(seq):
    pos = jnp.arange(CHUNK + seq, dtype=F32)
    freq_r = ROPE_THETA ** (-jnp.linspace(0.0, 1.0, RET_DK // 2, dtype=F32))
    ang_r = pos[:, None] * freq_r[None]
    cr = jnp.concatenate([jnp.cos(ang_r)] * 2, axis=-1)
    sr = jnp.concatenate([-jnp.sin(ang_r), jnp.sin(ang_r)], axis=-1)
    tok = jnp.arange(seq)
    row = (tok // GRID_W).astype(F32)
    col = (tok % GRID_W).astype(F32)
    n_pair = ATT_DH // 4
    freq_a = ROPE_THETA ** (-jnp.arange(n_pair, dtype=F32) / n_pair)
    ang_a = jnp.concatenate([row[:, None] * freq_a[None], col[:, None] * freq_a[None]], axis=-1)
    ca = jnp.concatenate([jnp.cos(ang_a)] * 2, axis=-1)
    sa = jnp.concatenate([-jnp.sin(ang_a), jnp.sin(ang_a)], axis=-1)
    return cr, sr, ca, sa


def _encode(x, wts, tabs, meta, *, attn_tq, attn_tk):
    batch, seq, _ = x.shape
    cr, sr, ca, sa = tabs
    proj_meta, ak_meta, vt_meta = meta
    x2 = x.reshape(batch * seq, D_MODEL)
    tm = 1024
    proj, qt, ak, vt = _in_proj(x2, wts["ln1"], wts["w_in"], cr, sr, ca, sa, wts["qg"],
                                wts["kg"], tm=tm, table_blocks=seq // tm)
    ret = _retention(proj, proj_meta, wts["lgf"], wts["lgb"], batch=batch, seq=seq)
    attend = lambda bounded: _attention(qt, ak, vt, ak_meta, vt_meta, wts["att_bound"],
                                        batch=batch, seq=seq, tq=attn_tq, tk=attn_tk,
                                        bounded=bounded)
    att = lax.cond(wts["att_bound"][0] <= ATT_BOUND_MAX,
                   lambda: attend(True), lambda: attend(False))
    h1, u2 = _out_proj(x2, ret, att, wts["w_out"], wts["ln2"], tm=512)
    y = _mlp(u2, h1, wts["w_up"], wts["w_down"], wts["fg"], tm=512, tf=1024)
    return y.reshape(batch, seq, D_MODEL)


def kernel(x_prompt, x_sample, meta_tokens, ln1_g, w_in, q_norm_g, k_norm_g, ret_log_decay_fwd,
           ret_log_decay_bwd, w_out, ln2_g, w_up, w_down, final_norm_g):
    assert w_in.shape[0] == 1, "meta-token residual stream is only skippable for a single layer"
    wi = w_in[0]
    w_in_p = jnp.concatenate([
        _deinterleave_cols(wi[:, OFF_RQ:OFF_RV], 2 * RET_HEADS),
        wi[:, OFF_RV:OFF_AQ],
        _deinterleave_cols(wi[:, OFF_AQ:OFF_AV], ATT_HEADS + ATT_KV_HEADS),
        wi[:, OFF_AV:],
    ], axis=1).astype(BF16)
    deint_vec = lambda g: g.reshape(64, 2).T.reshape(1, 128)
    wts = {
        "ln1": ln1_g[0].reshape(1, D_MODEL),
        "w_in": w_in_p,
        "qg": deint_vec(q_norm_g[0]),
        "kg": deint_vec(k_norm_g[0]),
        "lgf": ret_log_decay_fwd[0],
        "lgb": ret_log_decay_bwd[0],
        "w_out": w_out[0].astype(BF16),
        "ln2": ln2_g[0].reshape(1, D_MODEL),
        "w_up": w_up[0].astype(BF16),
        "w_down": w_down[0].astype(BF16),
        "fg": final_norm_g.reshape(1, D_MODEL),
    }
    wts["att_bound"] = (1.02 * LOG2E * ATT_DH ** 0.5 * jnp.max(jnp.abs(q_norm_g[0]))
                        * jnp.max(jnp.abs(k_norm_g[0]))).reshape(1).astype(F32)
    max_seq = max(x_prompt.shape[1], x_sample.shape[1])
    cr, sr, ca, sa = _rope_tables(max_seq)

    x_meta = jnp.concatenate(
        [jnp.zeros((CHUNK - N_META, D_MODEL), F32), meta_tokens.astype(F32)], axis=0)
    ones = jnp.ones((CHUNK, 128), F32)
    proj_meta, _, ak_meta, vt_meta = _in_proj(
        x_meta, wts["ln1"], w_in_p, cr[:CHUNK], sr[:CHUNK], ones, 0.0 * ones,
        wts["qg"], wts["kg"], tm=CHUNK, table_blocks=1)
    meta = (proj_meta, ak_meta, vt_meta)

    tabs = (cr[CHUNK:], sr[CHUNK:], ca, sa)
    y_prompt = _encode(x_prompt, wts, tabs, meta, attn_tq=512, attn_tk=1024)
    y_sample = _encode(x_sample, wts, tabs, meta, attn_tq=512, attn_tk=1024)
    return (y_prompt, y_sample)
```

```python
import jax
import jax.numpy as jnp
from jax import lax
from jax.experimental import pallas as pl
from jax.experimental.pallas import tpu as pltpu

F32 = jnp.float32
BF16 = jnp.bfloat16

D_MODEL = 2048
N_META = 16
GRID_W = 64
CHUNK = 128
RET_HEADS = 4
RET_DK = 128
RET_DV = 256
RET_QK = RET_HEADS * RET_DK
RET_V = RET_HEADS * RET_DV
ATT_DH = 128
ATT_HEADS = 8
ATT_KV_HEADS = 2
ATT_GROUP = ATT_HEADS // ATT_KV_HEADS
ATT_Q = ATT_HEADS * ATT_DH
ATT_KV = ATT_KV_HEADS * ATT_DH
D_FF = 4 * D_MODEL
IN_W = 2 * RET_QK + 2 * RET_V + ATT_Q + 2 * ATT_KV
ROPE_THETA = 10000.0
EPS = 1e-6
LOG2E = 1.4426950408889634

OFF_RQ = 0
OFF_RK = OFF_RQ + RET_QK
OFF_RV = OFF_RK + RET_QK
OFF_RG = OFF_RV + RET_V
OFF_AQ = OFF_RG + RET_V
OFF_AK = OFF_AQ + ATT_Q
OFF_AV = OFF_AK + ATT_KV

PROJ_TN = 512
N_RET_BLOCKS = OFF_AQ // PROJ_TN
VMEM_LIMIT = 56 * 1024 * 1024

_NT = (((1,), (1,)), ((), ()))
_TN = (((0,), (0,)), ((), ()))


def _rope_halves(a, c, s):
    outs = []
    for h in range(a.shape[1] // 128):
        ah = a[:, h * 128:(h + 1) * 128]
        outs.append(ah * c + pltpu.roll(ah, 64, 1) * s)
    return outs[0] if len(outs) == 1 else jnp.concatenate(outs, axis=1)


def _head_norm(a, g):
    outs = []
    for h in range(a.shape[1] // 128):
        ah = a[:, h * 128:(h + 1) * 128]
        ms = jnp.mean(ah * ah, axis=-1, keepdims=True)
        outs.append(ah * lax.rsqrt(ms + EPS) * g)
    return outs[0] if len(outs) == 1 else jnp.concatenate(outs, axis=1)


def _in_proj_kernel(x_ref, ln_ref, w_ref, cr_ref, sr_ref, ca_ref, sa_ref, qg_ref, kg_ref,
                    o_ref, qt_ref, ak_ref, vt_ref, u_ref):
    j = pl.program_id(1)

    @pl.when(j == 0)
    def _():
        x = x_ref[...]
        ms = jnp.mean(x * x, axis=-1, keepdims=True)
        u_ref[...] = (x * lax.rsqrt(ms + EPS) * ln_ref[...]).astype(BF16)

    acc = jnp.dot(u_ref[...], w_ref[...], preferred_element_type=F32)

    @pl.when(j == OFF_RQ // PROJ_TN)
    def _():
        o_ref[...] = _rope_halves(acc, cr_ref[...], sr_ref[...]).astype(BF16)

    @pl.when(j == OFF_RK // PROJ_TN)
    def _():
        o_ref[...] = (_rope_halves(acc, cr_ref[...], sr_ref[...]) * (RET_DK ** -0.5)).astype(BF16)

    @pl.when((j >= OFF_RV // PROJ_TN) & (j < N_RET_BLOCKS))
    def _():
        o_ref[...] = acc.astype(BF16)

    @pl.when((j >= OFF_AQ // PROJ_TN) & (j < OFF_AK // PROJ_TN))
    def _():
        qn = _head_norm(acc, qg_ref[...])
        q = _rope_halves(qn, ca_ref[...], sa_ref[...]) * (ATT_DH ** -0.5 * LOG2E)
        qt_ref[...] = q.T.astype(BF16)

    @pl.when(j == OFF_AK // PROJ_TN)
    def _():
        kn = _head_norm(acc[:, :ATT_KV], kg_ref[...])
        ak_ref[...] = _rope_halves(kn, ca_ref[...], sa_ref[...]).astype(BF16)
        vt_ref[...] = acc[:, ATT_KV:].T.astype(BF16)


def _in_proj(x, ln1, w_in, cr, sr, ca, sa, qg, kg, *, tm, table_blocks):
    t = x.shape[0]
    n_j = IN_W // PROJ_TN
    j_aq = OFF_AQ // PROJ_TN
    tab = pl.BlockSpec((tm, 128), lambda i, j: (i % table_blocks, 0))
    vec = lambda n: pl.BlockSpec((1, n), lambda i, j: (0, 0))
    return pl.pallas_call(
        _in_proj_kernel,
        out_shape=(jax.ShapeDtypeStruct((t, OFF_AQ), BF16),
                   jax.ShapeDtypeStruct((ATT_Q, t), BF16),
                   jax.ShapeDtypeStruct((t, ATT_KV), BF16),
                   jax.ShapeDtypeStruct((t // tm, ATT_KV, tm), BF16)),
        grid=(t // tm, n_j),
        in_specs=[
            pl.BlockSpec((tm, D_MODEL), lambda i, j: (i, 0)),
            vec(D_MODEL),
            pl.BlockSpec((D_MODEL, PROJ_TN), lambda i, j: (0, j)),
            tab, tab, tab, tab,
            vec(128), vec(128),
        ],
        out_specs=(
            pl.BlockSpec((tm, PROJ_TN), lambda i, j: (i, jnp.minimum(j, N_RET_BLOCKS - 1))),
            pl.BlockSpec((PROJ_TN, tm), lambda i, j: (jnp.clip(j - j_aq, 0, 1), i)),
            pl.BlockSpec((tm, ATT_KV), lambda i, j: (i, 0)),
            pl.BlockSpec((None, ATT_KV, tm), lambda i, j: (i, 0, 0)),
        ),
        scratch_shapes=[pltpu.VMEM((tm, D_MODEL), BF16)],
        compiler_params=pltpu.CompilerParams(
            dimension_semantics=("arbitrary", "arbitrary"), vmem_limit_bytes=VMEM_LIMIT),
        name="in_proj",
    )(x, ln1, w_in, cr, sr, ca, sa, qg, kg)


RET_STEP_CHUNKS = 4


def _chunk_pos():
    return lax.broadcasted_iota(jnp.int32, (CHUNK, 1), 0).astype(F32)


def _ret_bwd_kernel(lgf_ref, lgb_ref, q_ref, k_ref, v_ref, part_ref, s_ref, d_ref):
    b = pl.program_id(0)
    i = pl.program_id(1)

    @pl.when((b == 0) & (i == 0))
    def _():
        r = lax.broadcasted_iota(jnp.int32, (CHUNK, CHUNK), 0)
        c = lax.broadcasted_iota(jnp.int32, (CHUNK, CHUNK), 1)
        diff = (r - c).astype(F32)
        for h in range(RET_HEADS):
            df = jnp.where(diff >= 0, jnp.exp(lgf_ref[h] * jnp.maximum(diff, 0.0)), 0.0)
            db = jnp.where(diff < 0, jnp.exp(lgb_ref[h] * jnp.maximum(-diff, 0.0)), 0.0)
            d_ref[h] = df + db

    @pl.when(i == 0)
    def _():
        s_ref[...] = jnp.zeros_like(s_ref)

    pos = _chunk_pos()
    n_chunks = q_ref.shape[0] // CHUNK
    for h in range(RET_HEADS):
        lg = lgb_ref[h]
        q_dec = jnp.exp(lg * (float(CHUNK) - pos))
        k_dec = jnp.exp(lg * pos)
        c_dec = jnp.exp(lg * jnp.full((CHUNK, 1), float(CHUNK), F32))
        state = s_ref[h]
        for cc in reversed(range(n_chunks)):
            rs = slice(cc * CHUNK, (cc + 1) * CHUNK)
            q = q_ref[rs, h * RET_DK:(h + 1) * RET_DK]
            k = k_ref[rs, h * RET_DK:(h + 1) * RET_DK]
            v = v_ref[rs, h * RET_DV:(h + 1) * RET_DV]
            sc = lax.dot_general(q, k, _NT, preferred_element_type=F32) * d_ref[h]
            inner = jnp.dot(sc.astype(BF16), v, preferred_element_type=F32)
            cross = jnp.dot(q, state.astype(BF16), preferred_element_type=F32) * q_dec
            part_ref[rs, h * RET_DV:(h + 1) * RET_DV] = inner + cross
            kd = (k.astype(F32) * k_dec).astype(BF16)
            state = state * c_dec + lax.dot_general(kd, v, _TN, preferred_element_type=F32)
        s_ref[h] = state


def _ret_fwd_kernel(lgf_ref, q_ref, k_ref, v_ref, g_ref, km_ref, vm_ref, part_ref,
                    o_ref, s_ref):
    i = pl.program_id(1)
    pos = _chunk_pos()

    @pl.when(i == 0)
    def _():
        for h in range(RET_HEADS):
            k_dec = jnp.exp(lgf_ref[h] * (float(CHUNK - 1) - pos))
            kd = (km_ref[:, h * RET_DK:(h + 1) * RET_DK].astype(F32) * k_dec).astype(BF16)
            s_ref[h] = lax.dot_general(kd, vm_ref[:, h * RET_DV:(h + 1) * RET_DV], _TN,
                                       preferred_element_type=F32)

    n_chunks = q_ref.shape[0] // CHUNK
    for h in range(RET_HEADS):
        lg = lgf_ref[h]
        q_dec = jnp.exp(lg * (pos + 1.0))
        k_dec = jnp.exp(lg * (float(CHUNK - 1) - pos))
        c_dec = jnp.exp(lg * jnp.full((CHUNK, 1), float(CHUNK), F32))
        state = s_ref[h]
        for cc in range(n_chunks):
            rs = slice(cc * CHUNK, (cc + 1) * CHUNK)
            q = q_ref[rs, h * RET_DK:(h + 1) * RET_DK]
            k = k_ref[rs, h * RET_DK:(h + 1) * RET_DK]
            v = v_ref[rs, h * RET_DV:(h + 1) * RET_DV]
            cross = jnp.dot(q, state.astype(BF16), preferred_element_type=F32) * q_dec
            tot = part_ref[rs, h * RET_DV:(h + 1) * RET_DV] + cross
            ms = jnp.mean(tot * tot, axis=-1, keepdims=True)
            g = g_ref[rs, h * RET_DV:(h + 1) * RET_DV].astype(F32)
            gate = g / (1.0 + jnp.exp(-g))
            o_ref[rs, h * RET_DV:(h + 1) * RET_DV] = (
                tot * lax.rsqrt(ms + EPS) * gate).astype(BF16)
            kd = (k.astype(F32) * k_dec).astype(BF16)
            state = state * c_dec + lax.dot_general(kd, v, _TN, preferred_element_type=F32)
        s_ref[h] = state


def _retention(proj, proj_meta, lgf, lgb, *, batch, seq):
    t = proj.shape[0]
    rows_blk = RET_STEP_CHUNKS * CHUNK
    n = seq // rows_blk
    smem = pl.BlockSpec(memory_space=pltpu.SMEM)
    params = pltpu.CompilerParams(dimension_semantics=("arbitrary", "arbitrary"),
                                  vmem_limit_bytes=VMEM_LIMIT)

    def rows_rev(b, i):
        return b * n + (n - 1 - i)

    part = pl.pallas_call(
        _ret_bwd_kernel,
        out_shape=jax.ShapeDtypeStruct((t, RET_V), F32),
        grid=(batch, n),
        in_specs=[
            smem, smem,
            pl.BlockSpec((rows_blk, RET_QK), lambda b, i: (rows_rev(b, i), OFF_RQ // RET_QK)),
            pl.BlockSpec((rows_blk, RET_QK), lambda b, i: (rows_rev(b, i), OFF_RK // RET_QK)),
            pl.BlockSpec((rows_blk, RET_V), lambda b, i: (rows_rev(b, i), OFF_RV // RET_V)),
        ],
        out_specs=pl.BlockSpec((rows_blk, RET_V), lambda b, i: (rows_rev(b, i), 0)),
        scratch_shapes=[pltpu.VMEM((RET_HEADS, RET_DK, RET_DV), F32),
                        pltpu.VMEM((RET_HEADS, CHUNK, CHUNK), F32)],
        compiler_params=params,
        name="retention_bwd",
    )(lgf, lgb, proj, proj, proj)

    def rows(b, i):
        return b * n + i

    return pl.pallas_call(
        _ret_fwd_kernel,
        out_shape=jax.ShapeDtypeStruct((t, RET_V), BF16),
        grid=(batch, n),
        in_specs=[
            smem,
            pl.BlockSpec((rows_blk, RET_QK), lambda b, i: (rows(b, i), OFF_RQ // RET_QK)),
            pl.BlockSpec((rows_blk, RET_QK), lambda b, i: (rows(b, i), OFF_RK // RET_QK)),
            pl.BlockSpec((rows_blk, RET_V), lambda b, i: (rows(b, i), OFF_RV // RET_V)),
            pl.BlockSpec((rows_blk, RET_V), lambda b, i: (rows(b, i), OFF_RG // RET_V)),
            pl.BlockSpec((CHUNK, RET_QK), lambda b, i: (0, OFF_RK // RET_QK)),
            pl.BlockSpec((CHUNK, RET_V), lambda b, i: (0, OFF_RV // RET_V)),
            pl.BlockSpec((rows_blk, RET_V), lambda b, i: (rows(b, i), 0)),
        ],
        out_specs=pl.BlockSpec((rows_blk, RET_V), lambda b, i: (rows(b, i), 0)),
        scratch_shapes=[pltpu.VMEM((RET_HEADS, RET_DK, RET_DV), F32)],
        compiler_params=params,
        name="retention_fwd",
    )(lgf, proj, proj, proj, proj, proj_meta, proj_meta, part)


ATT_SUB = 256
ATT_KB = 1024
ATT_BOUND_MAX = 60.0


def _attn_subtile(k, vt, mask, qt_ref, m_ref, l_ref, acc_ref, g, c):
    qs = slice(c * ATT_SUB, (c + 1) * ATT_SUB)
    st = jnp.dot(k, qt_ref[g * ATT_DH:(g + 1) * ATT_DH, qs], preferred_element_type=F32)
    if mask is not None:
        st = jnp.where(mask, st, -jnp.inf)
    m_prev = m_ref[g, :, qs]
    m_new = jnp.maximum(m_prev, jnp.max(st, axis=0, keepdims=True))
    alpha = jnp.exp2(m_prev - m_new)
    p = jnp.exp2(st - m_new)
    l_ref[g, :, qs] = alpha * l_ref[g, :, qs] + jnp.sum(p, axis=0, keepdims=True)
    acc_ref[g, :, qs] = alpha * acc_ref[g, :, qs] + jnp.dot(
        vt, p.astype(BF16), preferred_element_type=F32)
    m_ref[g, :, qs] = m_new


def _attn_kernel(qt_ref, k_ref, vt_ref, km_ref, vtm_ref, o_ref, m_ref, l_ref, acc_ref):
    ki = pl.program_id(3)
    tq = qt_ref.shape[1]
    tk = k_ref.shape[0]

    @pl.when(ki == 0)
    def _():
        m_ref[...] = jnp.full_like(m_ref, -jnp.inf)
        l_ref[...] = jnp.zeros_like(l_ref)
        acc_ref[...] = jnp.zeros_like(acc_ref)
        is_meta = lax.broadcasted_iota(jnp.int32, (CHUNK, ATT_SUB), 0) >= CHUNK - N_META
        for g in range(ATT_GROUP):
            for c in range(tq // ATT_SUB):
                _attn_subtile(km_ref[...], vtm_ref[...], is_meta, qt_ref, m_ref, l_ref, acc_ref,
                              g, c)

    for r in range(tk // ATT_SUB):
        ks = slice(r * ATT_SUB, (r + 1) * ATT_SUB)
        for g in range(ATT_GROUP):
            for c in range(tq // ATT_SUB):
                _attn_subtile(k_ref[ks, :], vt_ref[:, ks], None, qt_ref, m_ref, l_ref, acc_ref,
                              g, c)

    @pl.when(ki == pl.num_programs(3) - 1)
    def _():
        for g in range(ATT_GROUP):
            o = acc_ref[g] / l_ref[g]
            o_ref[:, g * ATT_DH:(g + 1) * ATT_DH] = o.T.astype(BF16)


def _attn_bounded_kernel(bound_ref, qt_ref, k_ref, vt_ref, km_ref, vtm_ref, o_ref, l_ref, acc_ref):
    tq = qt_ref.shape[1]
    bound = bound_ref[0]
    subs = [slice(c * ATT_SUB, (c + 1) * ATT_SUB) for c in range(tq // ATT_SUB)]

    def update(k, vt, heads, first):
        tiles = [(g, qs) for g in heads for qs in subs]
        sts = [jnp.dot(k, qt_ref[g * ATT_DH:(g + 1) * ATT_DH, qs], preferred_element_type=F32)
               for g, qs in tiles]
        ps = [jnp.exp2(st - bound) for st in sts]
        for (g, qs), p in zip(tiles, ps):
            l = p.reshape(-1, 8, ATT_SUB).sum(axis=0)
            l_ref[g, :, qs] = l if first else l_ref[g, :, qs] + l
        for (g, qs), p in zip(tiles, ps):
            a = jnp.dot(vt, p.astype(BF16), preferred_element_type=F32)
            acc_ref[g, :, qs] = a if first else acc_ref[g, :, qs] + a

    update(km_ref[...], vtm_ref[...], range(ATT_GROUP), True)

    def key_block(kb, carry):
        k = k_ref[pl.ds(pl.multiple_of(kb * ATT_KB, ATT_KB), ATT_KB), :]
        vt = vt_ref[kb]
        for g in range(ATT_GROUP):
            update(k, vt, [g], False)
        return carry

    lax.fori_loop(0, vt_ref.shape[0], key_block, 0)

    for g in range(ATT_GROUP):
        o = acc_ref[g] / jnp.sum(l_ref[g], axis=0, keepdims=True)
        o_ref[:, g * ATT_DH:(g + 1) * ATT_DH] = o.T.astype(BF16)


def _attention_bounded(qt, ak, vt, ak_meta, vt_meta, bound, *, batch, seq, tq):
    t = ak.shape[0]
    nq = seq // tq
    nkb = seq // ATT_KB
    qw = ATT_GROUP * ATT_DH
    km = ak_meta[CHUNK - N_META:]
    vtm = vt_meta[0][:, CHUNK - N_META:]
    return pl.pallas_call(
        _attn_bounded_kernel,
        out_shape=jax.ShapeDtypeStruct((t, ATT_Q), BF16),
        grid=(batch, ATT_KV_HEADS, nq),
        in_specs=[
            pl.BlockSpec(memory_space=pltpu.SMEM),
            pl.BlockSpec((qw, tq), lambda b, h, qi: (h, b * nq + qi)),
            pl.BlockSpec((seq, ATT_DH), lambda b, h, qi: (b, h)),
            pl.BlockSpec((nkb, ATT_DH, ATT_KB), lambda b, h, qi: (b, h, 0)),
            pl.BlockSpec((N_META, ATT_DH), lambda b, h, qi: (0, h)),
            pl.BlockSpec((ATT_DH, N_META), lambda b, h, qi: (h, 0)),
        ],
        out_specs=pl.BlockSpec((tq, qw), lambda b, h, qi: (b * nq + qi, h)),
        scratch_shapes=[pltpu.VMEM((ATT_GROUP, 8, tq), F32),
                        pltpu.VMEM((ATT_GROUP, ATT_DH, tq), F32)],
        compiler_params=pltpu.CompilerParams(
            dimension_semantics=("arbitrary",) * 3, vmem_limit_bytes=VMEM_LIMIT),
        name="attention_bounded",
    )(bound, qt, ak, vt, km, vtm)


def _attention_online(qt, ak, vt, ak_meta, vt_meta, *, batch, seq, tq):
    t = ak.shape[0]
    nq = seq // tq
    nk = seq // ATT_KB
    qw = ATT_GROUP * ATT_DH
    return pl.pallas_call(
        _attn_kernel,
        out_shape=jax.ShapeDtypeStruct((t, ATT_Q), BF16),
        grid=(batch, ATT_KV_HEADS, nq, nk),
        in_specs=[
            pl.BlockSpec((qw, tq), lambda b, h, qi, ki: (h, b * nq + qi)),
            pl.BlockSpec((ATT_KB, ATT_DH), lambda b, h, qi, ki: (b * nk + ki, h)),
            pl.BlockSpec((None, ATT_DH, ATT_KB), lambda b, h, qi, ki: (b * nk + ki, h, 0)),
            pl.BlockSpec((CHUNK, ATT_DH), lambda b, h, qi, ki: (0, h)),
            pl.BlockSpec((None, ATT_DH, CHUNK), lambda b, h, qi, ki: (0, h, 0)),
        ],
        out_specs=pl.BlockSpec((tq, qw), lambda b, h, qi, ki: (b * nq + qi, h)),
        scratch_shapes=[pltpu.VMEM((ATT_GROUP, 1, tq), F32),
                        pltpu.VMEM((ATT_GROUP, 1, tq), F32),
                        pltpu.VMEM((ATT_GROUP, ATT_DH, tq), F32)],
        compiler_params=pltpu.CompilerParams(
            dimension_semantics=("arbitrary",) * 4, vmem_limit_bytes=VMEM_LIMIT),
        name="attention_online",
    )(qt, ak, vt, ak_meta, vt_meta)


def _out_proj_kernel(x_ref, r_ref, a_ref, wr_ref, wa_ref, ln_ref, h_ref, u_ref):
    h = (x_ref[...]
         + jnp.dot(r_ref[...], wr_ref[...], preferred_element_type=F32)
         + jnp.dot(a_ref[...], wa_ref[...], preferred_element_type=F32))
    h_ref[...] = h
    ms = jnp.mean(h * h, axis=-1, keepdims=True)
    u_ref[...] = (h * lax.rsqrt(ms + EPS) * ln_ref[...]).astype(BF16)


def _out_proj(x, ret, att, w_out, ln2, *, tm):
    t = x.shape[0]
    row = lambda w: pl.BlockSpec((tm, w), lambda i: (i, 0))
    return pl.pallas_call(
        _out_proj_kernel,
        out_shape=(jax.ShapeDtypeStruct((t, D_MODEL), F32),
                   jax.ShapeDtypeStruct((t, D_MODEL), BF16)),
        grid=(t // tm,),
        in_specs=[
            row(D_MODEL), row(RET_V), row(ATT_Q),
            pl.BlockSpec((RET_V, D_MODEL), lambda i: (0, 0)),
            pl.BlockSpec((ATT_Q, D_MODEL), lambda i: (1, 0)),
            pl.BlockSpec((1, D_MODEL), lambda i: (0, 0)),
        ],
        out_specs=(row(D_MODEL), row(D_MODEL)),
        compiler_params=pltpu.CompilerParams(
            dimension_semantics=("arbitrary",), vmem_limit_bytes=VMEM_LIMIT),
        name="out_proj",
    )(x, ret, att, w_out, w_out, ln2)


def _mlp_kernel(u_ref, h_ref, wu_ref, wd_ref, fg_ref, o_ref):
    j = pl.program_id(1)

    @pl.when(j == 0)
    def _():
        o_ref[...] = h_ref[...]

    a = jnp.dot(u_ref[...], wu_ref[...], preferred_element_type=F32)
    a = jnp.square(jnp.maximum(a, 0.0)).astype(BF16)
    o_ref[...] += jnp.dot(a, wd_ref[...], preferred_element_type=F32)

    @pl.when(j == pl.num_programs(1) - 1)
    def _():
        h = o_ref[...]
        ms = jnp.mean(h * h, axis=-1, keepdims=True)
        o_ref[...] = h * lax.rsqrt(ms + EPS) * fg_ref[...]


def _mlp(u, h, w_up, w_down, fg, *, tm, tf):
    t = u.shape[0]
    return pl.pallas_call(
        _mlp_kernel,
        out_shape=jax.ShapeDtypeStruct((t, D_MODEL), F32),
        grid=(t // tm, D_FF // tf),
        in_specs=[
            pl.BlockSpec((tm, D_MODEL), lambda i, j: (i, 0)),
            pl.BlockSpec((tm, D_MODEL), lambda i, j: (i, 0)),
            pl.BlockSpec((D_MODEL, tf), lambda i, j: (0, j)),
            pl.BlockSpec((tf, D_MODEL), lambda i, j: (j, 0)),
            pl.BlockSpec((1, D_MODEL), lambda i, j: (0, 0)),
        ],
        out_specs=pl.BlockSpec((tm, D_MODEL), lambda i, j: (i, 0)),
        compiler_params=pltpu.CompilerParams(
            dimension_semantics=("arbitrary", "arbitrary"), vmem_limit_bytes=VMEM_LIMIT),
        name="mlp",
    )(u, h, w_up, w_down, fg)


def _deinterleave_cols(w, n_heads):
    d = w.shape[0]
    return w.reshape(d, n_heads, 64, 2).transpose(0, 1, 3, 2).reshape(d, n_heads * 128)


def _rope_tables(seq):
    pos = jnp.arange(CHUNK + seq, dtype=F32)
    freq_r = ROPE_THETA ** (-jnp.linspace(0.0, 1.0, RET_DK // 2, dtype=F32))
    ang_r = pos[:, None] * freq_r[None]
    cr = jnp.concatenate([jnp.cos(ang_r)] * 2, axis=-1)
    sr = jnp.concatenate([-jnp.sin(ang_r), jnp.sin(ang_r)], axis=-1)
    tok = jnp.arange(seq)
    row = (tok // GRID_W).astype(F32)
    col = (tok % GRID_W).astype(F32)
    n_pair = ATT_DH // 4
    freq_a = ROPE_THETA ** (-jnp.arange(n_pair, dtype=F32) / n_pair)
    ang_a = jnp.concatenate([row[:, None] * freq_a[None], col[:, None] * freq_a[None]], axis=-1)
    ca = jnp.concatenate([jnp.cos(ang_a)] * 2, axis=-1)
    sa = jnp.concatenate([-jnp.sin(ang_a), jnp.sin(ang_a)], axis=-1)
    return cr, sr, ca, sa


def _encode(x, wts, tabs, meta, *, attn_tq):
    batch, seq, _ = x.shape
    cr, sr, ca, sa = tabs
    proj_meta, ak_meta, vt_meta = meta
    x2 = x.reshape(batch * seq, D_MODEL)
    tm = ATT_KB
    proj, qt, ak, vt = _in_proj(x2, wts["ln1"], wts["w_in"], cr, sr, ca, sa, wts["qg"],
                                wts["kg"], tm=tm, table_blocks=seq // tm)
    ret = _retention(proj, proj_meta, wts["lgf"], wts["lgb"], batch=batch, seq=seq)
    att = lax.cond(
        wts["att_bound"][0] <= ATT_BOUND_MAX,
        lambda: _attention_bounded(qt, ak, vt, ak_meta, vt_meta, wts["att_bound"],
                                   batch=batch, seq=seq, tq=attn_tq),
        lambda: _attention_online(qt, ak, vt, ak_meta, vt_meta, batch=batch, seq=seq, tq=attn_tq))
    h1, u2 = _out_proj(x2, ret, att, wts["w_out"], wts["ln2"], tm=512)
    y = _mlp(u2, h1, wts["w_up"], wts["w_down"], wts["fg"], tm=512, tf=1024)
    return y.reshape(batch, seq, D_MODEL)


def kernel(x_prompt, x_sample, meta_tokens, ln1_g, w_in, q_norm_g, k_norm_g, ret_log_decay_fwd,
           ret_log_decay_bwd, w_out, ln2_g, w_up, w_down, final_norm_g):
    assert w_in.shape[0] == 1, "meta-token residual stream is only skippable for a single layer"
    wi = w_in[0]
    w_in_p = jnp.concatenate([
        _deinterleave_cols(wi[:, OFF_RQ:OFF_RV], 2 * RET_HEADS),
        wi[:, OFF_RV:OFF_AQ],
        _deinterleave_cols(wi[:, OFF_AQ:OFF_AV], ATT_HEADS + ATT_KV_HEADS),
        wi[:, OFF_AV:],
    ], axis=1).astype(BF16)
    deint_vec = lambda g: g.reshape(64, 2).T.reshape(1, 128)
    wts = {
        "ln1": ln1_g[0].reshape(1, D_MODEL),
        "w_in": w_in_p,
        "qg": deint_vec(q_norm_g[0]),
        "kg": deint_vec(k_norm_g[0]),
        "lgf": ret_log_decay_fwd[0],
        "lgb": ret_log_decay_bwd[0],
        "w_out": w_out[0].astype(BF16),
        "ln2": ln2_g[0].reshape(1, D_MODEL),
        "w_up": w_up[0].astype(BF16),
        "w_down": w_down[0].astype(BF16),
        "fg": final_norm_g.reshape(1, D_MODEL),
    }
    wts["att_bound"] = (1.02 * LOG2E * ATT_DH ** 0.5 * jnp.max(jnp.abs(q_norm_g[0]))
                        * jnp.max(jnp.abs(k_norm_g[0]))).reshape(1).astype(F32)
    max_seq = max(x_prompt.shape[1], x_sample.shape[1])
    cr, sr, ca, sa = _rope_tables(max_seq)

    x_meta = jnp.concatenate(
        [jnp.zeros((CHUNK - N_META, D_MODEL), F32), meta_tokens.astype(F32)], axis=0)
    ones = jnp.ones((CHUNK, 128), F32)
    proj_meta, _, ak_meta, vt_meta = _in_proj(
        x_meta, wts["ln1"], w_in_p, cr[:CHUNK], sr[:CHUNK], ones, 0.0 * ones,
        wts["qg"], wts["kg"], tm=CHUNK, table_blocks=1)
    meta = (proj_meta, ak_meta, vt_meta)

    tabs = (cr[CHUNK:], sr[CHUNK:], ca, sa)
    y_prompt = _encode(x_prompt, wts, tabs, meta, attn_tq=512)
    y_sample = _encode(x_sample, wts, tabs, meta, attn_tq=512)
    return (y_prompt, y_sample)
```

```python
import jax
import jax.numpy as jnp
from jax import lax
from jax.experimental import pallas as pl
from jax.experimental.pallas import tpu as pltpu

F32 = jnp.float32
BF16 = jnp.bfloat16

D_MODEL = 2048
N_META = 16
GRID_W = 64
CHUNK = 128
RET_HEADS = 4
RET_DK = 128
RET_DV = 256
RET_QK = RET_HEADS * RET_DK
RET_V = RET_HEADS * RET_DV
ATT_DH = 128
ATT_HEADS = 8
ATT_KV_HEADS = 2
ATT_GROUP = ATT_HEADS // ATT_KV_HEADS
ATT_Q = ATT_HEADS * ATT_DH
ATT_KV = ATT_KV_HEADS * ATT_DH
D_FF = 4 * D_MODEL
IN_W = 2 * RET_QK + 2 * RET_V + ATT_Q + 2 * ATT_KV
ROPE_THETA = 10000.0
EPS = 1e-6
LOG2E = 1.4426950408889634

OFF_RQ = 0
OFF_RK = OFF_RQ + RET_QK
OFF_RV = OFF_RK + RET_QK
OFF_RG = OFF_RV + RET_V
OFF_AQ = OFF_RG + RET_V
OFF_AK = OFF_AQ + ATT_Q
OFF_AV = OFF_AK + ATT_KV

PROJ_TN = 512
N_RET_BLOCKS = OFF_AQ // PROJ_TN
IN_PROJ_SUB_ROWS = 256
VMEM_LIMIT = 56 * 1024 * 1024

_NT = (((1,), (1,)), ((), ()))
_TN = (((0,), (0,)), ((), ()))


def _rope_halves(a, c, s):
    outs = []
    for h in range(a.shape[1] // 128):
        ah = a[:, h * 128:(h + 1) * 128]
        outs.append(ah * c + pltpu.roll(ah, 64, 1) * s)
    return outs[0] if len(outs) == 1 else jnp.concatenate(outs, axis=1)


def _head_norm(a, g):
    outs = []
    for h in range(a.shape[1] // 128):
        ah = a[:, h * 128:(h + 1) * 128]
        ms = jnp.mean(ah * ah, axis=-1, keepdims=True)
        outs.append(ah * lax.rsqrt(ms + EPS) * g)
    return outs[0] if len(outs) == 1 else jnp.concatenate(outs, axis=1)


def _in_proj_kernel(x_ref, ln_ref, w_ref, cr_ref, sr_ref, ca_ref, sa_ref, qg_ref, kg_ref,
                    o_ref, qt_ref, ak_ref, vt_ref, u_ref):
    j = pl.program_id(1)
    tm = x_ref.shape[0]
    sub = min(tm, IN_PROJ_SUB_ROWS)

    def run(epilogue, normalize=False):
        for r in range(tm // sub):
            rs = slice(r * sub, (r + 1) * sub)
            if normalize:
                x = x_ref[rs, :]
                ms = jnp.mean(x * x, axis=-1, keepdims=True)
                u_ref[rs, :] = (x * lax.rsqrt(ms + EPS) * ln_ref[...]).astype(BF16)
            acc = jnp.dot(u_ref[rs, :], w_ref[...], preferred_element_type=F32)
            epilogue(acc, rs)

    def ret_q(acc, rs):
        o_ref[rs, :] = _rope_halves(acc, cr_ref[rs, :], sr_ref[rs, :]).astype(BF16)

    def ret_k(acc, rs):
        o_ref[rs, :] = (_rope_halves(acc, cr_ref[rs, :], sr_ref[rs, :])
                        * (RET_DK ** -0.5)).astype(BF16)

    def plain(acc, rs):
        o_ref[rs, :] = acc.astype(BF16)

    def att_q(acc, rs):
        qn = _head_norm(acc, qg_ref[...])
        q = _rope_halves(qn, ca_ref[rs, :], sa_ref[rs, :]) * (ATT_DH ** -0.5 * LOG2E)
        qt_ref[:, rs] = q.T.astype(BF16)

    def att_kv(acc, rs):
        kn = _head_norm(acc[:, :ATT_KV], kg_ref[...])
        ak_ref[rs, :] = _rope_halves(kn, ca_ref[rs, :], sa_ref[rs, :]).astype(BF16)
        vt_ref[:, rs] = acc[:, ATT_KV:].T.astype(BF16)

    pl.when(j == OFF_RQ // PROJ_TN)(lambda: run(ret_q, normalize=True))
    pl.when(j == OFF_RK // PROJ_TN)(lambda: run(ret_k))
    pl.when((j >= OFF_RV // PROJ_TN) & (j < N_RET_BLOCKS))(lambda: run(plain))
    pl.when((j >= OFF_AQ // PROJ_TN) & (j < OFF_AK // PROJ_TN))(lambda: run(att_q))
    pl.when(j == OFF_AK // PROJ_TN)(lambda: run(att_kv))


def _in_proj(x, ln1, w_in, cr, sr, ca, sa, qg, kg, *, tm, table_blocks):
    t = x.shape[0]
    n_j = IN_W // PROJ_TN
    j_aq = OFF_AQ // PROJ_TN
    tab = pl.BlockSpec((tm, 128), lambda i, j: (i % table_blocks, 0))
    vec = lambda n: pl.BlockSpec((1, n), lambda i, j: (0, 0))
    return pl.pallas_call(
        _in_proj_kernel,
        out_shape=(jax.ShapeDtypeStruct((t, OFF_AQ), BF16),
                   jax.ShapeDtypeStruct((ATT_Q, t), BF16),
                   jax.ShapeDtypeStruct((t, ATT_KV), BF16),
                   jax.ShapeDtypeStruct((t // tm, ATT_KV, tm), BF16)),
        grid=(t // tm, n_j),
        in_specs=[
            pl.BlockSpec((tm, D_MODEL), lambda i, j: (i, 0)),
            vec(D_MODEL),
            pl.BlockSpec((D_MODEL, PROJ_TN), lambda i, j: (0, j)),
            tab, tab, tab, tab,
            vec(128), vec(128),
        ],
        out_specs=(
            pl.BlockSpec((tm, PROJ_TN), lambda i, j: (i, jnp.minimum(j, N_RET_BLOCKS - 1))),
            pl.BlockSpec((PROJ_TN, tm), lambda i, j: (jnp.clip(j - j_aq, 0, 1), i)),
            pl.BlockSpec((tm, ATT_KV), lambda i, j: (i, 0)),
            pl.BlockSpec((None, ATT_KV, tm), lambda i, j: (i, 0, 0)),
        ),
        scratch_shapes=[pltpu.VMEM((tm, D_MODEL), BF16)],
        compiler_params=pltpu.CompilerParams(
            dimension_semantics=("arbitrary", "arbitrary"), vmem_limit_bytes=VMEM_LIMIT),
        name="in_proj",
    )(x, ln1, w_in, cr, sr, ca, sa, qg, kg)


RET_STEP_CHUNKS = 4


def _chunk_pos():
    return lax.broadcasted_iota(jnp.int32, (CHUNK, 1), 0).astype(F32)


def _ret_bwd_kernel(lgf_ref, lgb_ref, q_ref, k_ref, v_ref, part_ref, s_ref, d_ref):
    b = pl.program_id(0)
    i = pl.program_id(1)

    @pl.when((b == 0) & (i == 0))
    def _():
        r = lax.broadcasted_iota(jnp.int32, (CHUNK, CHUNK), 0)
        c = lax.broadcasted_iota(jnp.int32, (CHUNK, CHUNK), 1)
        diff = (r - c).astype(F32)
        for h in range(RET_HEADS):
            df = jnp.where(diff >= 0, jnp.exp(lgf_ref[h] * jnp.maximum(diff, 0.0)), 0.0)
            db = jnp.where(diff < 0, jnp.exp(lgb_ref[h] * jnp.maximum(-diff, 0.0)), 0.0)
            d_ref[h] = df + db

    @pl.when(i == 0)
    def _():
        s_ref[...] = jnp.zeros_like(s_ref)

    pos = _chunk_pos()
    n_chunks = q_ref.shape[0] // CHUNK
    for h in range(RET_HEADS):
        lg = lgb_ref[h]
        q_dec = jnp.exp(lg * (float(CHUNK) - pos))
        k_dec = jnp.exp(lg * pos)
        c_dec = jnp.exp(lg * jnp.full((CHUNK, 1), float(CHUNK), F32))
        state = s_ref[h]
        for cc in reversed(range(n_chunks)):
            rs = slice(cc * CHUNK, (cc + 1) * CHUNK)
            q = q_ref[rs, h * RET_DK:(h + 1) * RET_DK]
            k = k_ref[rs, h * RET_DK:(h + 1) * RET_DK]
            v = v_ref[rs, h * RET_DV:(h + 1) * RET_DV]
            sc = lax.dot_general(q, k, _NT, preferred_element_type=F32) * d_ref[h]
            inner = jnp.dot(sc.astype(BF16), v, preferred_element_type=F32)
            cross = jnp.dot(q, state.astype(BF16), preferred_element_type=F32) * q_dec
            part_ref[rs, h * RET_DV:(h + 1) * RET_DV] = inner + cross
            kd = (k.astype(F32) * k_dec).astype(BF16)
            state = state * c_dec + lax.dot_general(kd, v, _TN, preferred_element_type=F32)
        s_ref[h] = state


def _ret_fwd_kernel(lgf_ref, q_ref, k_ref, v_ref, g_ref, km_ref, vm_ref, part_ref,
                    o_ref, s_ref):
    i = pl.program_id(1)
    pos = _chunk_pos()

    @pl.when(i == 0)
    def _():
        for h in range(RET_HEADS):
            k_dec = jnp.exp(lgf_ref[h] * (float(CHUNK - 1) - pos))
            kd = (km_ref[:, h * RET_DK:(h + 1) * RET_DK].astype(F32) * k_dec).astype(BF16)
            s_ref[h] = lax.dot_general(kd, vm_ref[:, h * RET_DV:(h + 1) * RET_DV], _TN,
                                       preferred_element_type=F32)

    n_chunks = q_ref.shape[0] // CHUNK
    for h in range(RET_HEADS):
        lg = lgf_ref[h]
        q_dec = jnp.exp(lg * (pos + 1.0))
        k_dec = jnp.exp(lg * (float(CHUNK - 1) - pos))
        c_dec = jnp.exp(lg * jnp.full((CHUNK, 1), float(CHUNK), F32))
        state = s_ref[h]
        for cc in range(n_chunks):
            rs = slice(cc * CHUNK, (cc + 1) * CHUNK)
            q = q_ref[rs, h * RET_DK:(h + 1) * RET_DK]
            k = k_ref[rs, h * RET_DK:(h + 1) * RET_DK]
            v = v_ref[rs, h * RET_DV:(h + 1) * RET_DV]
            cross = jnp.dot(q, state.astype(BF16), preferred_element_type=F32) * q_dec
            tot = part_ref[rs, h * RET_DV:(h + 1) * RET_DV] + cross
            ms = jnp.mean(tot * tot, axis=-1, keepdims=True)
            g = g_ref[rs, h * RET_DV:(h + 1) * RET_DV].astype(F32)
            gate = g / (1.0 + jnp.exp(-g))
            o_ref[rs, h * RET_DV:(h + 1) * RET_DV] = (
                tot * lax.rsqrt(ms + EPS) * gate).astype(BF16)
            kd = (k.astype(F32) * k_dec).astype(BF16)
            state = state * c_dec + lax.dot_general(kd, v, _TN, preferred_element_type=F32)
        s_ref[h] = state


def _retention(proj, proj_meta, lgf, lgb, *, batch, seq):
    t = proj.shape[0]
    rows_blk = RET_STEP_CHUNKS * CHUNK
    n = seq // rows_blk
    smem = pl.BlockSpec(memory_space=pltpu.SMEM)
    params = pltpu.CompilerParams(dimension_semantics=("arbitrary", "arbitrary"),
                                  vmem_limit_bytes=VMEM_LIMIT)

    def rows_rev(b, i):
        return b * n + (n - 1 - i)

    part = pl.pallas_call(
        _ret_bwd_kernel,
        out_shape=jax.ShapeDtypeStruct((t, RET_V), F32),
        grid=(batch, n),
        in_specs=[
            smem, smem,
            pl.BlockSpec((rows_blk, RET_QK), lambda b, i: (rows_rev(b, i), OFF_RQ // RET_QK)),
            pl.BlockSpec((rows_blk, RET_QK), lambda b, i: (rows_rev(b, i), OFF_RK // RET_QK)),
            pl.BlockSpec((rows_blk, RET_V), lambda b, i: (rows_rev(b, i), OFF_RV // RET_V)),
        ],
        out_specs=pl.BlockSpec((rows_blk, RET_V), lambda b, i: (rows_rev(b, i), 0)),
        scratch_shapes=[pltpu.VMEM((RET_HEADS, RET_DK, RET_DV), F32),
                        pltpu.VMEM((RET_HEADS, CHUNK, CHUNK), F32)],
        compiler_params=params,
        name="retention_bwd",
    )(lgf, lgb, proj, proj, proj)

    def rows(b, i):
        return b * n + i

    return pl.pallas_call(
        _ret_fwd_kernel,
        out_shape=jax.ShapeDtypeStruct((t, RET_V), BF16),
        grid=(batch, n),
        in_specs=[
            smem,
            pl.BlockSpec((rows_blk, RET_QK), lambda b, i: (rows(b, i), OFF_RQ // RET_QK)),
            pl.BlockSpec((rows_blk, RET_QK), lambda b, i: (rows(b, i), OFF_RK // RET_QK)),
            pl.BlockSpec((rows_blk, RET_V), lambda b, i: (rows(b, i), OFF_RV // RET_V)),
            pl.BlockSpec((rows_blk, RET_V), lambda b, i: (rows(b, i), OFF_RG // RET_V)),
            pl.BlockSpec((CHUNK, RET_QK), lambda b, i: (0, OFF_RK // RET_QK)),
            pl.BlockSpec((CHUNK, RET_V), lambda b, i: (0, OFF_RV // RET_V)),
            pl.BlockSpec((rows_blk, RET_V), lambda b, i: (rows(b, i), 0)),
        ],
        out_specs=pl.BlockSpec((rows_blk, RET_V), lambda b, i: (rows(b, i), 0)),
        scratch_shapes=[pltpu.VMEM((RET_HEADS, RET_DK, RET_DV), F32)],
        compiler_params=params,
        name="retention_fwd",
    )(lgf, proj, proj, proj, proj, proj_meta, proj_meta, part)


ATT_SUB = 256
ATT_KB = 1024
ATT_BOUND_MAX = 60.0


def _attn_subtile(k, vt, mask, qt_ref, m_ref, l_ref, acc_ref, g, c):
    qs = slice(c * ATT_SUB, (c + 1) * ATT_SUB)
    st = jnp.dot(k, qt_ref[g * ATT_DH:(g + 1) * ATT_DH, qs], preferred_element_type=F32)
    if mask is not None:
        st = jnp.where(mask, st, -jnp.inf)
    m_prev = m_ref[g, :, qs]
    m_new = jnp.maximum(m_prev, jnp.max(st, axis=0, keepdims=True))
    alpha = jnp.exp2(m_prev - m_new)
    p = jnp.exp2(st - m_new)
    l_ref[g, :, qs] = alpha * l_ref[g, :, qs] + jnp.sum(p, axis=0, keepdims=True)
    acc_ref[g, :, qs] = alpha * acc_ref[g, :, qs] + jnp.dot(
        vt, p.astype(BF16), preferred_element_type=F32)
    m_ref[g, :, qs] = m_new


def _attn_kernel(qt_ref, k_ref, vt_ref, km_ref, vtm_ref, o_ref, m_ref, l_ref, acc_ref):
    ki = pl.program_id(3)
    tq = qt_ref.shape[1]
    tk = k_ref.shape[0]

    @pl.when(ki == 0)
    def _():
        m_ref[...] = jnp.full_like(m_ref, -jnp.inf)
        l_ref[...] = jnp.zeros_like(l_ref)
        acc_ref[...] = jnp.zeros_like(acc_ref)
        is_meta = lax.broadcasted_iota(jnp.int32, (CHUNK, ATT_SUB), 0) >= CHUNK - N_META
        for g in range(ATT_GROUP):
            for c in range(tq // ATT_SUB):
                _attn_subtile(km_ref[...], vtm_ref[...], is_meta, qt_ref, m_ref, l_ref, acc_ref,
                              g, c)

    for r in range(tk // ATT_SUB):
        ks = slice(r * ATT_SUB, (r + 1) * ATT_SUB)
        for g in range(ATT_GROUP):
            for c in range(tq // ATT_SUB):
                _attn_subtile(k_ref[ks, :], vt_ref[:, ks], None, qt_ref, m_ref, l_ref, acc_ref,
                              g, c)

    @pl.when(ki == pl.num_programs(3) - 1)
    def _():
        for g in range(ATT_GROUP):
            o = acc_ref[g] / l_ref[g]
            o_ref[:, g * ATT_DH:(g + 1) * ATT_DH] = o.T.astype(BF16)


def _attn_bounded_kernel(bound_ref, qt_ref, k_ref, vt_ref, km_ref, vtm_ref, o_ref, l_ref, acc_ref):
    tq = qt_ref.shape[1]
    bound = bound_ref[0]
    subs = [slice(c * ATT_SUB, (c + 1) * ATT_SUB) for c in range(tq // ATT_SUB)]

    def update(k, vt, heads, first):
        tiles = [(g, qs) for g in heads for qs in subs]
        sts = [jnp.dot(k, qt_ref[g * ATT_DH:(g + 1) * ATT_DH, qs], preferred_element_type=F32)
               for g, qs in tiles]
        ps = [jnp.exp2(st - bound) for st in sts]
        for (g, qs), p in zip(tiles, ps):
            l = p.reshape(-1, 8, ATT_SUB).sum(axis=0)
            l_ref[g, :, qs] = l if first else l_ref[g, :, qs] + l
        for (g, qs), p in zip(tiles, ps):
            a = jnp.dot(vt, p.astype(BF16), preferred_element_type=F32)
            acc_ref[g, :, qs] = a if first else acc_ref[g, :, qs] + a

    update(km_ref[...], vtm_ref[...], range(ATT_GROUP), True)

    def key_block(kb, carry):
        k = k_ref[pl.ds(pl.multiple_of(kb * ATT_KB, ATT_KB), ATT_KB), :]
        vt = vt_ref[kb]
        update(k, vt, range(ATT_GROUP), False)
        return carry

    lax.fori_loop(0, vt_ref.shape[0], key_block, 0)

    for g in range(ATT_GROUP):
        o = acc_ref[g] / jnp.sum(l_ref[g], axis=0, keepdims=True)
        o_ref[:, g * ATT_DH:(g + 1) * ATT_DH] = o.T.astype(BF16)


def _attention_bounded(qt, ak, vt, ak_meta, vt_meta, bound, *, batch, seq, tq):
    t = ak.shape[0]
    nq = seq // tq
    nkb = seq // ATT_KB
    qw = ATT_GROUP * ATT_DH
    km = ak_meta[CHUNK - N_META:]
    vtm = vt_meta[0][:, CHUNK - N_META:]
    return pl.pallas_call(
        _attn_bounded_kernel,
        out_shape=jax.ShapeDtypeStruct((t, ATT_Q), BF16),
        grid=(batch, ATT_KV_HEADS, nq),
        in_specs=[
            pl.BlockSpec(memory_space=pltpu.SMEM),
            pl.BlockSpec((qw, tq), lambda b, h, qi: (h, b * nq + qi)),
            pl.BlockSpec((seq, ATT_DH), lambda b, h, qi: (b, h)),
            pl.BlockSpec((nkb, ATT_DH, ATT_KB), lambda b, h, qi: (b, h, 0)),
            pl.BlockSpec((N_META, ATT_DH), lambda b, h, qi: (0, h)),
            pl.BlockSpec((ATT_DH, N_META), lambda b, h, qi: (h, 0)),
        ],
        out_specs=pl.BlockSpec((tq, qw), lambda b, h, qi: (b * nq + qi, h)),
        scratch_shapes=[pltpu.VMEM((ATT_GROUP, 8, tq), F32),
                        pltpu.VMEM((ATT_GROUP, ATT_DH, tq), F32)],
        compiler_params=pltpu.CompilerParams(
            dimension_semantics=("arbitrary",) * 3, vmem_limit_bytes=VMEM_LIMIT),
        name="attention_bounded",
    )(bound, qt, ak, vt, km, vtm)


def _attention_online(qt, ak, vt, ak_meta, vt_meta, *, batch, seq, tq):
    t = ak.shape[0]
    nq = seq // tq
    nk = seq // ATT_KB
    qw = ATT_GROUP * ATT_DH
    return pl.pallas_call(
        _attn_kernel,
        out_shape=jax.ShapeDtypeStruct((t, ATT_Q), BF16),
        grid=(batch, ATT_KV_HEADS, nq, nk),
        in_specs=[
            pl.BlockSpec((qw, tq), lambda b, h, qi, ki: (h, b * nq + qi)),
            pl.BlockSpec((ATT_KB, ATT_DH), lambda b, h, qi, ki: (b * nk + ki, h)),
            pl.BlockSpec((None, ATT_DH, ATT_KB), lambda b, h, qi, ki: (b * nk + ki, h, 0)),
            pl.BlockSpec((CHUNK, ATT_DH), lambda b, h, qi, ki: (0, h)),
            pl.BlockSpec((None, ATT_DH, CHUNK), lambda b, h, qi, ki: (0, h, 0)),
        ],
        out_specs=pl.BlockSpec((tq, qw), lambda b, h, qi, ki: (b * nq + qi, h)),
        scratch_shapes=[pltpu.VMEM((ATT_GROUP, 1, tq), F32),
                        pltpu.VMEM((ATT_GROUP, 1, tq), F32),
                        pltpu.VMEM((ATT_GROUP, ATT_DH, tq), F32)],
        compiler_params=pltpu.CompilerParams(
            dimension_semantics=("arbitrary",) * 4, vmem_limit_bytes=VMEM_LIMIT),
        name="attention_online",
    )(qt, ak, vt, ak_meta, vt_meta)


def _out_proj_kernel(x_ref, r_ref, a_ref, wr_ref, wa_ref, ln_ref, h_ref, u_ref):
    h = (x_ref[...]
         + jnp.dot(r_ref[...], wr_ref[...], preferred_element_type=F32)
         + jnp.dot(a_ref[...], wa_ref[...], preferred_element_type=F32))
    h_ref[...] = h
    ms = jnp.mean(h * h, axis=-1, keepdims=True)
    u_ref[...] = (h * lax.rsqrt(ms + EPS) * ln_ref[...]).astype(BF16)


def _out_proj(x, ret, att, w_out, ln2, *, tm):
    t = x.shape[0]
    row = lambda w: pl.BlockSpec((tm, w), lambda i: (i, 0))
    return pl.pallas_call(
        _out_proj_kernel,
        out_shape=(jax.ShapeDtypeStruct((t, D_MODEL), F32),
                   jax.ShapeDtypeStruct((t, D_MODEL), BF16)),
        grid=(t // tm,),
        in_specs=[
            row(D_MODEL), row(RET_V), row(ATT_Q),
            pl.BlockSpec((RET_V, D_MODEL), lambda i: (0, 0)),
            pl.BlockSpec((ATT_Q, D_MODEL), lambda i: (1, 0)),
            pl.BlockSpec((1, D_MODEL), lambda i: (0, 0)),
        ],
        out_specs=(row(D_MODEL), row(D_MODEL)),
        compiler_params=pltpu.CompilerParams(
            dimension_semantics=("arbitrary",), vmem_limit_bytes=VMEM_LIMIT),
        name="out_proj",
    )(x, ret, att, w_out, w_out, ln2)


def _mlp_kernel(u_ref, h_ref, wu_ref, wd_ref, fg_ref, o_ref):
    j = pl.program_id(1)

    @pl.when(j == 0)
    def _():
        o_ref[...] = h_ref[...]

    a = jnp.dot(u_ref[...], wu_ref[...], preferred_element_type=F32)
    a = jnp.square(jnp.maximum(a, 0.0)).astype(BF16)
    o_ref[...] += jnp.dot(a, wd_ref[...], preferred_element_type=F32)

    @pl.when(j == pl.num_programs(1) - 1)
    def _():
        h = o_ref[...]
        ms = jnp.mean(h * h, axis=-1, keepdims=True)
        o_ref[...] = h * lax.rsqrt(ms + EPS) * fg_ref[...]


def _mlp(u, h, w_up, w_down, fg, *, tm, tf):
    t = u.shape[0]
    return pl.pallas_call(
        _mlp_kernel,
        out_shape=jax.ShapeDtypeStruct((t, D_MODEL), F32),
        grid=(t // tm, D_FF // tf),
        in_specs=[
            pl.BlockSpec((tm, D_MODEL), lambda i, j: (i, 0)),
            pl.BlockSpec((tm, D_MODEL), lambda i, j: (i, 0)),
            pl.BlockSpec((D_MODEL, tf), lambda i, j: (0, j)),
            pl.BlockSpec((tf, D_MODEL), lambda i, j: (j, 0)),
            pl.BlockSpec((1, D_MODEL), lambda i, j: (0, 0)),
        ],
        out_specs=pl.BlockSpec((tm, D_MODEL), lambda i, j: (i, 0)),
        compiler_params=pltpu.CompilerParams(
            dimension_semantics=("arbitrary", "arbitrary"), vmem_limit_bytes=VMEM_LIMIT),
        name="mlp",
    )(u, h, w_up, w_down, fg)


def _deinterleave_cols(w, n_heads):
    d = w.shape[0]
    return w.reshape(d, n_heads, 64, 2).transpose(0, 1, 3, 2).reshape(d, n_heads * 128)


def _rope_tables(seq):
    n_chunks = seq // CHUNK + 1
    freq_r = ROPE_THETA ** (-jnp.linspace(0.0, 1.0, RET_DK // 2, dtype=F32))
    ang_hi = (jnp.arange(n_chunks, dtype=F32) * float(CHUNK))[:, None] * freq_r[None]
    ang_lo = jnp.arange(CHUNK, dtype=F32)[:, None] * freq_r[None]
    c_hi, s_hi = jnp.cos(ang_hi)[:, None], jnp.sin(ang_hi)[:, None]
    c_lo, s_lo = jnp.cos(ang_lo)[None], jnp.sin(ang_lo)[None]
    cos_r = (c_hi * c_lo - s_hi * s_lo).reshape(n_chunks * CHUNK, RET_DK // 2)
    sin_r = (s_hi * c_lo + c_hi * s_lo).reshape(n_chunks * CHUNK, RET_DK // 2)
    cr = jnp.concatenate([cos_r, cos_r], axis=-1)
    sr = jnp.concatenate([-sin_r, sin_r], axis=-1)
    n_pair = ATT_DH // 4
    freq_a = ROPE_THETA ** (-jnp.arange(n_pair, dtype=F32) / n_pair)
    rows_n = seq // GRID_W
    ang_row = jnp.arange(rows_n, dtype=F32)[:, None] * freq_a[None]
    ang_col = jnp.arange(GRID_W, dtype=F32)[:, None] * freq_a[None]

    def axial(fn):
        by_row = jnp.broadcast_to(fn(ang_row)[:, None], (rows_n, GRID_W, n_pair))
        by_col = jnp.broadcast_to(fn(ang_col)[None], (rows_n, GRID_W, n_pair))
        return jnp.concatenate([by_row, by_col], axis=-1).reshape(seq, 2 * n_pair)

    cos_a, sin_a = axial(jnp.cos), axial(jnp.sin)
    ca = jnp.concatenate([cos_a, cos_a], axis=-1)
    sa = jnp.concatenate([-sin_a, sin_a], axis=-1)
    return cr, sr, ca, sa


def _encode(x, wts, tabs, meta, *, attn_tq):
    batch, seq, _ = x.shape
    cr, sr, ca, sa = tabs
    proj_meta, ak_meta, vt_meta = meta
    x2 = x.reshape(batch * seq, D_MODEL)
    tm = ATT_KB
    proj, qt, ak, vt = _in_proj(x2, wts["ln1"], wts["w_in"], cr, sr, ca, sa, wts["qg"],
                                wts["kg"], tm=tm, table_blocks=seq // tm)
    ret = _retention(proj, proj_meta, wts["lgf"], wts["lgb"], batch=batch, seq=seq)
    att = lax.cond(
        wts["att_bound"][0] <= ATT_BOUND_MAX,
        lambda: _attention_bounded(qt, ak, vt, ak_meta, vt_meta, wts["att_bound"],
                                   batch=batch, seq=seq, tq=attn_tq),
        lambda: _attention_online(qt, ak, vt, ak_meta, vt_meta, batch=batch, seq=seq, tq=attn_tq))
    h1, u2 = _out_proj(x2, ret, att, wts["w_out"], wts["ln2"], tm=512)
    y = _mlp(u2, h1, wts["w_up"], wts["w_down"], wts["fg"], tm=512, tf=1024)
    return y.reshape(batch, seq, D_MODEL)


def kernel(x_prompt, x_sample, meta_tokens, ln1_g, w_in, q_norm_g, k_norm_g, ret_log_decay_fwd,
           ret_log_decay_bwd, w_out, ln2_g, w_up, w_down, final_norm_g):
    assert w_in.shape[0] == 1, "meta-token residual stream is only skippable for a single layer"
    wi = w_in[0]
    w_in_p = jnp.concatenate([
        _deinterleave_cols(wi[:, OFF_RQ:OFF_RV], 2 * RET_HEADS),
        wi[:, OFF_RV:OFF_AQ],
        _deinterleave_cols(wi[:, OFF_AQ:OFF_AV], ATT_HEADS + ATT_KV_HEADS),
        wi[:, OFF_AV:],
    ], axis=1).astype(BF16)
    deint_vec = lambda g: g.reshape(64, 2).T.reshape(1, 128)
    wts = {
        "ln1": ln1_g[0].reshape(1, D_MODEL),
        "w_in": w_in_p,
        "qg": deint_vec(q_norm_g[0]),
        "kg": deint_vec(k_norm_g[0]),
        "lgf": ret_log_decay_fwd[0],
        "lgb": ret_log_decay_bwd[0],
        "w_out": w_out[0].astype(BF16),
        "ln2": ln2_g[0].reshape(1, D_MODEL),
        "w_up": w_up[0].astype(BF16),
        "w_down": w_down[0].astype(BF16),
        "fg": final_norm_g.reshape(1, D_MODEL),
    }
    wts["att_bound"] = (1.02 * LOG2E * ATT_DH ** 0.5 * jnp.max(jnp.abs(q_norm_g[0]))
                        * jnp.max(jnp.abs(k_norm_g[0]))).reshape(1).astype(F32)
    max_seq = max(x_prompt.shape[1], x_sample.shape[1])
    cr, sr, ca, sa = _rope_tables(max_seq)

    x_meta = jnp.concatenate(
        [jnp.zeros((CHUNK - N_META, D_MODEL), F32), meta_tokens.astype(F32)], axis=0)
    ones = jnp.ones((CHUNK, 128), F32)
    proj_meta, _, ak_meta, vt_meta = _in_proj(
        x_meta, wts["ln1"], w_in_p, cr[:CHUNK], sr[:CHUNK], ones, 0.0 * ones,
        wts["qg"], wts["kg"], tm=CHUNK, table_blocks=1)
    meta = (proj_meta, ak_meta, vt_meta)

    tabs = (cr[CHUNK:], sr[CHUNK:], ca, sa)
    y_prompt = _encode(x_prompt, wts, tabs, meta, attn_tq=512)
    y_sample = _encode(x_sample, wts, tabs, meta, attn_tq=512)
    return (y_prompt, y_sample)
```

```python
import jax
import jax.numpy as jnp
from jax import lax
from jax.experimental import pallas as pl
from jax.experimental.pallas import tpu as pltpu

F32 = jnp.float32
BF16 = jnp.bfloat16

D_MODEL = 2048
N_META = 16
GRID_W = 64
CHUNK = 128
RET_HEADS = 4
RET_DK = 128
RET_DV = 256
RET_QK = RET_HEADS * RET_DK
RET_V = RET_HEADS * RET_DV
ATT_DH = 128
ATT_HEADS = 8
ATT_KV_HEADS = 2
ATT_GROUP = ATT_HEADS // ATT_KV_HEADS
ATT_Q = ATT_HEADS * ATT_DH
ATT_KV = ATT_KV_HEADS * ATT_DH
D_FF = 4 * D_MODEL
IN_W = 2 * RET_QK + 2 * RET_V + ATT_Q + 2 * ATT_KV
ROPE_THETA = 10000.0
EPS = 1e-6
LOG2E = 1.4426950408889634

OFF_RQ = 0
OFF_RK = OFF_RQ + RET_QK
OFF_RV = OFF_RK + RET_QK
OFF_RG = OFF_RV + RET_V
OFF_AQ = OFF_RG + RET_V
OFF_AK = OFF_AQ + ATT_Q
OFF_AV = OFF_AK + ATT_KV

PROJ_TN = 512
N_RET_BLOCKS = OFF_AQ // PROJ_TN
IN_PROJ_SUB_ROWS = 256
VMEM_LIMIT = 56 * 1024 * 1024

_NT = (((1,), (1,)), ((), ()))
_TN = (((0,), (0,)), ((), ()))


def _rope_halves(a, c, s):
    outs = []
    for h in range(a.shape[1] // 128):
        ah = a[:, h * 128:(h + 1) * 128]
        outs.append(ah * c + pltpu.roll(ah, 64, 1) * s)
    return outs[0] if len(outs) == 1 else jnp.concatenate(outs, axis=1)


def _head_norm(a, g):
    outs = []
    for h in range(a.shape[1] // 128):
        ah = a[:, h * 128:(h + 1) * 128]
        ms = jnp.mean(ah * ah, axis=-1, keepdims=True)
        outs.append(ah * lax.rsqrt(ms + EPS) * g)
    return outs[0] if len(outs) == 1 else jnp.concatenate(outs, axis=1)


def _in_proj_kernel(x_hbm, ln_ref, w_ref, rhi_ref, rlo_ref, arow_ref, acol_ref, qg_ref, kg_ref,
                    o_ref, qt_ref, ak_ref, vt_ref, u_ref, x_ref, x_sem):
    j = pl.program_id(1)
    tm = u_ref.shape[0]
    sub = min(tm, IN_PROJ_SUB_ROWS)

    def ret_tables(rs):
        cs, ss = [], []
        for a in range(rs.start // CHUNK, rs.stop // CHUNK):
            hc, hs = rhi_ref[0, a:a + 1, :], rhi_ref[1, a:a + 1, :]
            cs.append(hc * rlo_ref[0] - hs * rlo_ref[1])
            ss.append(hs * rlo_ref[2] + hc * rlo_ref[3])
        return jnp.concatenate(cs, axis=0), jnp.concatenate(ss, axis=0)

    def axial_tables(rs):
        rows = range(rs.start // GRID_W, rs.stop // GRID_W)
        return (jnp.concatenate([arow_ref[0, g:g + 1, :] + acol_ref[0] for g in rows], axis=0),
                jnp.concatenate([arow_ref[1, g:g + 1, :] + acol_ref[1] for g in rows], axis=0))

    def run(epilogue, normalize=False):
        for r in range(tm // sub):
            rs = slice(r * sub, (r + 1) * sub)
            if normalize:
                x = x_ref[rs, :]
                ms = jnp.mean(x * x, axis=-1, keepdims=True)
                u_ref[rs, :] = (x * lax.rsqrt(ms + EPS) * ln_ref[...]).astype(BF16)
            acc = jnp.dot(u_ref[rs, :], w_ref[...], preferred_element_type=F32)
            epilogue(acc, rs)

    def ret_q(acc, rs):
        o_ref[rs, :] = _rope_halves(acc, *ret_tables(rs)).astype(BF16)

    def ret_k(acc, rs):
        o_ref[rs, :] = (_rope_halves(acc, *ret_tables(rs)) * (RET_DK ** -0.5)).astype(BF16)

    def plain(acc, rs):
        o_ref[rs, :] = acc.astype(BF16)

    def att_q(acc, rs):
        qn = _head_norm(acc, qg_ref[...])
        q = _rope_halves(qn, *axial_tables(rs)) * (ATT_DH ** -0.5 * LOG2E)
        qt_ref[:, rs] = q.T.astype(BF16)

    def att_kv(acc, rs):
        kn = _head_norm(acc[:, :ATT_KV], kg_ref[...])
        ak_ref[rs, :] = _rope_halves(kn, *axial_tables(rs)).astype(BF16)
        vt_ref[:, rs] = acc[:, ATT_KV:].T.astype(BF16)

    i = pl.program_id(0)

    def x_copy(tile):
        return pltpu.make_async_copy(x_hbm.at[pl.ds(tile * tm, tm), :], x_ref, x_sem)

    @pl.when(j == OFF_RQ // PROJ_TN)
    def _():
        pl.when(i == 0)(lambda: x_copy(0).start())
        x_copy(i).wait()
        run(ret_q, normalize=True)

    @pl.when(j == OFF_RK // PROJ_TN)
    def _():
        pl.when(i + 1 < pl.num_programs(0))(lambda: x_copy(i + 1).start())
        run(ret_k)

    pl.when((j >= OFF_RV // PROJ_TN) & (j < N_RET_BLOCKS))(lambda: run(plain))
    pl.when((j >= OFF_AQ // PROJ_TN) & (j < OFF_AK // PROJ_TN))(lambda: run(att_q))
    pl.when(j == OFF_AK // PROJ_TN)(lambda: run(att_kv))


def _in_proj(x, ln1, w_in, rhi, rlo, arow, acol, qg, kg, *, tm, table_blocks):
    t = x.shape[0]
    n_j = IN_W // PROJ_TN
    j_aq = OFF_AQ // PROJ_TN
    seq_tab = lambda a, n: pl.BlockSpec((a.shape[0], n, 128), lambda i, j: (0, i % table_blocks, 0))
    whole = lambda a: pl.BlockSpec(a.shape, lambda i, j: (0, 0, 0))
    vec = lambda n: pl.BlockSpec((1, n), lambda i, j: (0, 0))
    return pl.pallas_call(
        _in_proj_kernel,
        out_shape=(jax.ShapeDtypeStruct((t, OFF_AQ), BF16),
                   jax.ShapeDtypeStruct((ATT_Q, t), BF16),
                   jax.ShapeDtypeStruct((t, ATT_KV), BF16),
                   jax.ShapeDtypeStruct((t // tm, ATT_KV, tm), BF16)),
        grid=(t // tm, n_j),
        in_specs=[
            pl.BlockSpec(memory_space=pl.ANY),
            vec(D_MODEL),
            pl.BlockSpec((D_MODEL, PROJ_TN), lambda i, j: (0, j)),
            seq_tab(rhi, tm // CHUNK), whole(rlo), seq_tab(arow, tm // GRID_W), whole(acol),
            vec(128), vec(128),
        ],
        out_specs=(
            pl.BlockSpec((tm, PROJ_TN), lambda i, j: (i, jnp.minimum(j, N_RET_BLOCKS - 1))),
            pl.BlockSpec((PROJ_TN, tm), lambda i, j: (jnp.clip(j - j_aq, 0, 1), i)),
            pl.BlockSpec((tm, ATT_KV), lambda i, j: (i, 0)),
            pl.BlockSpec((None, ATT_KV, tm), lambda i, j: (i, 0, 0)),
        ),
        scratch_shapes=[pltpu.VMEM((tm, D_MODEL), BF16), pltpu.VMEM((tm, D_MODEL), F32),
                        pltpu.SemaphoreType.DMA],
        compiler_params=pltpu.CompilerParams(
            dimension_semantics=("arbitrary", "arbitrary"), vmem_limit_bytes=VMEM_LIMIT),
        name="in_proj",
    )(x, ln1, w_in, rhi, rlo, arow, acol, qg, kg)


RET_STEP_CHUNKS = 4


def _chunk_pos():
    return lax.broadcasted_iota(jnp.int32, (CHUNK, 1), 0).astype(F32)


def _ret_bwd_kernel(lgf_ref, lgb_ref, q_ref, k_ref, v_ref, part_ref, s_ref, d_ref):
    b = pl.program_id(0)
    i = pl.program_id(1)

    @pl.when((b == 0) & (i == 0))
    def _():
        r = lax.broadcasted_iota(jnp.int32, (CHUNK, CHUNK), 0)
        c = lax.broadcasted_iota(jnp.int32, (CHUNK, CHUNK), 1)
        diff = (r - c).astype(F32)
        for h in range(RET_HEADS):
            df = jnp.where(diff >= 0, jnp.exp(lgf_ref[h] * jnp.maximum(diff, 0.0)), 0.0)
            db = jnp.where(diff < 0, jnp.exp(lgb_ref[h] * jnp.maximum(-diff, 0.0)), 0.0)
            d_ref[h] = df + db

    @pl.when(i == 0)
    def _():
        s_ref[...] = jnp.zeros_like(s_ref)

    pos = _chunk_pos()
    n_chunks = q_ref.shape[0] // CHUNK
    for h in range(RET_HEADS):
        lg = lgb_ref[h]
        q_dec = jnp.exp(lg * (float(CHUNK) - pos))
        k_dec = jnp.exp(lg * pos)
        c_dec = jnp.exp(lg * jnp.full((CHUNK, 1), float(CHUNK), F32))
        state = s_ref[h]
        for cc in reversed(range(n_chunks)):
            rs = slice(cc * CHUNK, (cc + 1) * CHUNK)
            q = q_ref[rs, h * RET_DK:(h + 1) * RET_DK]
            k = k_ref[rs, h * RET_DK:(h + 1) * RET_DK]
            v = v_ref[rs, h * RET_DV:(h + 1) * RET_DV]
            sc = lax.dot_general(q, k, _NT, preferred_element_type=F32) * d_ref[h]
            inner = jnp.dot(sc.astype(BF16), v, preferred_element_type=F32)
            cross = jnp.dot(q, state.astype(BF16), preferred_element_type=F32) * q_dec
            part_ref[rs, h * RET_DV:(h + 1) * RET_DV] = inner + cross
            kd = (k.astype(F32) * k_dec).astype(BF16)
            state = state * c_dec + lax.dot_general(kd, v, _TN, preferred_element_type=F32)
        s_ref[h] = state


def _ret_fwd_kernel(lgf_ref, q_ref, k_ref, v_ref, g_ref, km_ref, vm_ref, part_ref,
                    o_ref, s_ref):
    i = pl.program_id(1)
    pos = _chunk_pos()

    @pl.when(i == 0)
    def _():
        for h in range(RET_HEADS):
            k_dec = jnp.exp(lgf_ref[h] * (float(CHUNK - 1) - pos))
            kd = (km_ref[:, h * RET_DK:(h + 1) * RET_DK].astype(F32) * k_dec).astype(BF16)
            s_ref[h] = lax.dot_general(kd, vm_ref[:, h * RET_DV:(h + 1) * RET_DV], _TN,
                                       preferred_element_type=F32)

    n_chunks = q_ref.shape[0] // CHUNK
    for h in range(RET_HEADS):
        lg = lgf_ref[h]
        q_dec = jnp.exp(lg * (pos + 1.0))
        k_dec = jnp.exp(lg * (float(CHUNK - 1) - pos))
        c_dec = jnp.exp(lg * jnp.full((CHUNK, 1), float(CHUNK), F32))
        state = s_ref[h]
        for cc in range(n_chunks):
            rs = slice(cc * CHUNK, (cc + 1) * CHUNK)
            q = q_ref[rs, h * RET_DK:(h + 1) * RET_DK]
            k = k_ref[rs, h * RET_DK:(h + 1) * RET_DK]
            v = v_ref[rs, h * RET_DV:(h + 1) * RET_DV]
            cross = jnp.dot(q, state.astype(BF16), preferred_element_type=F32) * q_dec
            tot = part_ref[rs, h * RET_DV:(h + 1) * RET_DV] + cross
            ms = jnp.mean(tot * tot, axis=-1, keepdims=True)
            g = g_ref[rs, h * RET_DV:(h + 1) * RET_DV].astype(F32)
            gate = g / (1.0 + jnp.exp(-g))
            o_ref[rs, h * RET_DV:(h + 1) * RET_DV] = (
                tot * lax.rsqrt(ms + EPS) * gate).astype(BF16)
            kd = (k.astype(F32) * k_dec).astype(BF16)
            state = state * c_dec + lax.dot_general(kd, v, _TN, preferred_element_type=F32)
        s_ref[h] = state


def _retention(proj, proj_meta, lgf, lgb, *, batch, seq):
    t = proj.shape[0]
    rows_blk = RET_STEP_CHUNKS * CHUNK
    n = seq // rows_blk
    smem = pl.BlockSpec(memory_space=pltpu.SMEM)
    params = pltpu.CompilerParams(dimension_semantics=("arbitrary", "arbitrary"),
                                  vmem_limit_bytes=VMEM_LIMIT)

    def rows_rev(b, i):
        return b * n + (n - 1 - i)

    part = pl.pallas_call(
        _ret_bwd_kernel,
        out_shape=jax.ShapeDtypeStruct((t, RET_V), F32),
        grid=(batch, n),
        in_specs=[
            smem, smem,
            pl.BlockSpec((rows_blk, RET_QK), lambda b, i: (rows_rev(b, i), OFF_RQ // RET_QK)),
            pl.BlockSpec((rows_blk, RET_QK), lambda b, i: (rows_rev(b, i), OFF_RK // RET_QK)),
            pl.BlockSpec((rows_blk, RET_V), lambda b, i: (rows_rev(b, i), OFF_RV // RET_V)),
        ],
        out_specs=pl.BlockSpec((rows_blk, RET_V), lambda b, i: (rows_rev(b, i), 0)),
        scratch_shapes=[pltpu.VMEM((RET_HEADS, RET_DK, RET_DV), F32),
                        pltpu.VMEM((RET_HEADS, CHUNK, CHUNK), F32)],
        compiler_params=params,
        name="retention_bwd",
    )(lgf, lgb, proj, proj, proj)

    def rows(b, i):
        return b * n + i

    return pl.pallas_call(
        _ret_fwd_kernel,
        out_shape=jax.ShapeDtypeStruct((t, RET_V), BF16),
        grid=(batch, n),
        in_specs=[
            smem,
            pl.BlockSpec((rows_blk, RET_QK), lambda b, i: (rows(b, i), OFF_RQ // RET_QK)),
            pl.BlockSpec((rows_blk, RET_QK), lambda b, i: (rows(b, i), OFF_RK // RET_QK)),
            pl.BlockSpec((rows_blk, RET_V), lambda b, i: (rows(b, i), OFF_RV // RET_V)),
            pl.BlockSpec((rows_blk, RET_V), lambda b, i: (rows(b, i), OFF_RG // RET_V)),
            pl.BlockSpec((CHUNK, RET_QK), lambda b, i: (0, OFF_RK // RET_QK)),
            pl.BlockSpec((CHUNK, RET_V), lambda b, i: (0, OFF_RV // RET_V)),
            pl.BlockSpec((rows_blk, RET_V), lambda b, i: (rows(b, i), 0)),
        ],
        out_specs=pl.BlockSpec((rows_blk, RET_V), lambda b, i: (rows(b, i), 0)),
        scratch_shapes=[pltpu.VMEM((RET_HEADS, RET_DK, RET_DV), F32)],
        compiler_params=params,
        name="retention_fwd",
    )(lgf, proj, proj, proj, proj, proj_meta, proj_meta, part)


ATT_SUB = 256
ATT_KB = 1024
ATT_BOUND_MAX = 60.0


def _attn_subtile(k, vt, mask, qt_ref, m_ref, l_ref, acc_ref, g, c):
    qs = slice(c * ATT_SUB, (c + 1) * ATT_SUB)
    st = jnp.dot(k, qt_ref[g * ATT_DH:(g + 1) * ATT_DH, qs], preferred_element_type=F32)
    if mask is not None:
        st = jnp.where(mask, st, -jnp.inf)
    m_prev = m_ref[g, :, qs]
    m_new = jnp.maximum(m_prev, jnp.max(st, axis=0, keepdims=True))
    alpha = jnp.exp2(m_prev - m_new)
    p = jnp.exp2(st - m_new)
    l_ref[g, :, qs] = alpha * l_ref[g, :, qs] + jnp.sum(p, axis=0, keepdims=True)
    acc_ref[g, :, qs] = alpha * acc_ref[g, :, qs] + jnp.dot(
        vt, p.astype(BF16), preferred_element_type=F32)
    m_ref[g, :, qs] = m_new


def _attn_kernel(qt_ref, k_ref, vt_ref, km_ref, vtm_ref, o_ref, m_ref, l_ref, acc_ref):
    ki = pl.program_id(3)
    tq = qt_ref.shape[1]
    tk = k_ref.shape[0]

    @pl.when(ki == 0)
    def _():
        m_ref[...] = jnp.full_like(m_ref, -jnp.inf)
        l_ref[...] = jnp.zeros_like(l_ref)
        acc_ref[...] = jnp.zeros_like(acc_ref)
        is_meta = lax.broadcasted_iota(jnp.int32, (CHUNK, ATT_SUB), 0) >= CHUNK - N_META
        for g in range(ATT_GROUP):
            for c in range(tq // ATT_SUB):
                _attn_subtile(km_ref[...], vtm_ref[...], is_meta, qt_ref, m_ref, l_ref, acc_ref,
                              g, c)

    for r in range(tk // ATT_SUB):
        ks = slice(r * ATT_SUB, (r + 1) * ATT_SUB)
        for g in range(ATT_GROUP):
            for c in range(tq // ATT_SUB):
                _attn_subtile(k_ref[ks, :], vt_ref[:, ks], None, qt_ref, m_ref, l_ref, acc_ref,
                              g, c)

    @pl.when(ki == pl.num_programs(3) - 1)
    def _():
        for g in range(ATT_GROUP):
            o = acc_ref[g] / l_ref[g]
            o_ref[:, g * ATT_DH:(g + 1) * ATT_DH] = o.T.astype(BF16)


def _attn_bounded_kernel(bound_ref, qt_ref, k_ref, vt_ref, km_ref, vtm_ref, o_ref, l_ref, acc_ref):
    tq = qt_ref.shape[1]
    bound = bound_ref[0]
    subs = [slice(c * ATT_SUB, (c + 1) * ATT_SUB) for c in range(tq // ATT_SUB)]

    def update(k, vt, heads, first):
        tiles = [(g, qs) for g in heads for qs in subs]
        sts = [jnp.dot(k, qt_ref[g * ATT_DH:(g + 1) * ATT_DH, qs], preferred_element_type=F32)
               for g, qs in tiles]
        ps = [jnp.exp2(st - bound) for st in sts]
        for (g, qs), p in zip(tiles, ps):
            l = p.reshape(-1, 8, ATT_SUB).sum(axis=0)
            l_ref[g, :, qs] = l if first else l_ref[g, :, qs] + l
        for (g, qs), p in zip(tiles, ps):
            a = jnp.dot(vt, p.astype(BF16), preferred_element_type=F32)
            acc_ref[g, :, qs] = a if first else acc_ref[g, :, qs] + a

    update(km_ref[...], vtm_ref[...], range(ATT_GROUP), True)

    def key_block(kb, carry):
        k = k_ref[pl.ds(pl.multiple_of(kb * ATT_KB, ATT_KB), ATT_KB), :]
        vt = vt_ref[kb]
        update(k, vt, range(ATT_GROUP), False)
        return carry

    lax.fori_loop(0, vt_ref.shape[0], key_block, 0)

    for g in range(ATT_GROUP):
        o = acc_ref[g] / jnp.sum(l_ref[g], axis=0, keepdims=True)
        o_ref[:, g * ATT_DH:(g + 1) * ATT_DH] = o.T.astype(BF16)


def _attention_bounded(qt, ak, vt, ak_meta, vt_meta, bound, *, batch, seq, tq):
    t = ak.shape[0]
    nq = seq // tq
    nkb = seq // ATT_KB
    qw = ATT_GROUP * ATT_DH
    km = ak_meta[CHUNK - N_META:]
    vtm = vt_meta[0][:, CHUNK - N_META:]
    return pl.pallas_call(
        _attn_bounded_kernel,
        out_shape=jax.ShapeDtypeStruct((t, ATT_Q), BF16),
        grid=(batch, ATT_KV_HEADS, nq),
        in_specs=[
            pl.BlockSpec(memory_space=pltpu.SMEM),
            pl.BlockSpec((qw, tq), lambda b, h, qi: (h, b * nq + qi)),
            pl.BlockSpec((seq, ATT_DH), lambda b, h, qi: (b, h)),
            pl.BlockSpec((nkb, ATT_DH, ATT_KB), lambda b, h, qi: (b, h, 0)),
            pl.BlockSpec((N_META, ATT_DH), lambda b, h, qi: (0, h)),
            pl.BlockSpec((ATT_DH, N_META), lambda b, h, qi: (h, 0)),
        ],
        out_specs=pl.BlockSpec((tq, qw), lambda b, h, qi: (b * nq + qi, h)),
        scratch_shapes=[pltpu.VMEM((ATT_GROUP, 8, tq), F32),
                        pltpu.VMEM((ATT_GROUP, ATT_DH, tq), F32)],
        compiler_params=pltpu.CompilerParams(
            dimension_semantics=("arbitrary",) * 3, vmem_limit_bytes=VMEM_LIMIT),
        name="attention_bounded",
    )(bound, qt, ak, vt, km, vtm)


def _attention_online(qt, ak, vt, ak_meta, vt_meta, *, batch, seq, tq):
    t = ak.shape[0]
    nq = seq // tq
    nk = seq // ATT_KB
    qw = ATT_GROUP * ATT_DH
    return pl.pallas_call(
        _attn_kernel,
        out_shape=jax.ShapeDtypeStruct((t, ATT_Q), BF16),
        grid=(batch, ATT_KV_HEADS, nq, nk),
        in_specs=[
            pl.BlockSpec((qw, tq), lambda b, h, qi, ki: (h, b * nq + qi)),
            pl.BlockSpec((ATT_KB, ATT_DH), lambda b, h, qi, ki: (b * nk + ki, h)),
            pl.BlockSpec((None, ATT_DH, ATT_KB), lambda b, h, qi, ki: (b * nk + ki, h, 0)),
            pl.BlockSpec((CHUNK, ATT_DH), lambda b, h, qi, ki: (0, h)),
            pl.BlockSpec((None, ATT_DH, CHUNK), lambda b, h, qi, ki: (0, h, 0)),
        ],
        out_specs=pl.BlockSpec((tq, qw), lambda b, h, qi, ki: (b * nq + qi, h)),
        scratch_shapes=[pltpu.VMEM((ATT_GROUP, 1, tq), F32),
                        pltpu.VMEM((ATT_GROUP, 1, tq), F32),
                        pltpu.VMEM((ATT_GROUP, ATT_DH, tq), F32)],
        compiler_params=pltpu.CompilerParams(
            dimension_semantics=("arbitrary",) * 4, vmem_limit_bytes=VMEM_LIMIT),
        name="attention_online",
    )(qt, ak, vt, ak_meta, vt_meta)


def _out_proj_kernel(x_ref, r_ref, a_ref, wr_ref, wa_ref, ln_ref, h_ref, u_ref):
    h = (x_ref[...]
         + jnp.dot(r_ref[...], wr_ref[...], preferred_element_type=F32)
         + jnp.dot(a_ref[...], wa_ref[...], preferred_element_type=F32))
    h_ref[...] = h
    ms = jnp.mean(h * h, axis=-1, keepdims=True)
    u_ref[...] = (h * lax.rsqrt(ms + EPS) * ln_ref[...]).astype(BF16)


def _out_proj(x, ret, att, w_out, ln2, *, tm):
    t = x.shape[0]
    row = lambda w: pl.BlockSpec((tm, w), lambda i: (i, 0))
    return pl.pallas_call(
        _out_proj_kernel,
        out_shape=(jax.ShapeDtypeStruct((t, D_MODEL), F32),
                   jax.ShapeDtypeStruct((t, D_MODEL), BF16)),
        grid=(t // tm,),
        in_specs=[
            row(D_MODEL), row(RET_V), row(ATT_Q),
            pl.BlockSpec((RET_V, D_MODEL), lambda i: (0, 0)),
            pl.BlockSpec((ATT_Q, D_MODEL), lambda i: (1, 0)),
            pl.BlockSpec((1, D_MODEL), lambda i: (0, 0)),
        ],
        out_specs=(row(D_MODEL), row(D_MODEL)),
        compiler_params=pltpu.CompilerParams(
            dimension_semantics=("arbitrary",), vmem_limit_bytes=VMEM_LIMIT),
        name="out_proj",
    )(x, ret, att, w_out, w_out, ln2)


def _mlp_kernel(u_ref, h_ref, wu_ref, wd_ref, fg_ref, o_ref):
    j = pl.program_id(1)

    @pl.when(j == 0)
    def _():
        o_ref[...] = h_ref[...]

    a = jnp.dot(u_ref[...], wu_ref[...], preferred_element_type=F32)
    a = jnp.square(jnp.maximum(a, 0.0)).astype(BF16)
    o_ref[...] += jnp.dot(a, wd_ref[...], preferred_element_type=F32)

    @pl.when(j == pl.num_programs(1) - 1)
    def _():
        h = o_ref[...]
        ms = jnp.mean(h * h, axis=-1, keepdims=True)
        o_ref[...] = h * lax.rsqrt(ms + EPS) * fg_ref[...]


def _mlp(u, h, w_up, w_down, fg, *, tm, tf):
    t = u.shape[0]
    return pl.pallas_call(
        _mlp_kernel,
        out_shape=jax.ShapeDtypeStruct((t, D_MODEL), F32),
        grid=(t // tm, D_FF // tf),
        in_specs=[
            pl.BlockSpec((tm, D_MODEL), lambda i, j: (i, 0)),
            pl.BlockSpec((tm, D_MODEL), lambda i, j: (i, 0)),
            pl.BlockSpec((D_MODEL, tf), lambda i, j: (0, j)),
            pl.BlockSpec((tf, D_MODEL), lambda i, j: (j, 0)),
            pl.BlockSpec((1, D_MODEL), lambda i, j: (0, 0)),
        ],
        out_specs=pl.BlockSpec((tm, D_MODEL), lambda i, j: (i, 0)),
        compiler_params=pltpu.CompilerParams(
            dimension_semantics=("arbitrary", "arbitrary"), vmem_limit_bytes=VMEM_LIMIT),
        name="mlp",
    )(u, h, w_up, w_down, fg)


def _deinterleave_cols(w, n_heads):
    d = w.shape[0]
    return w.reshape(d, n_heads, 64, 2).transpose(0, 1, 3, 2).reshape(d, n_heads * 128)


def _rope_tables(seq):
    sign = jnp.concatenate([-jnp.ones((64,), F32), jnp.ones((64,), F32)])
    n_chunks = seq // CHUNK + 1
    freq_r = ROPE_THETA ** (-jnp.linspace(0.0, 1.0, RET_DK // 2, dtype=F32))
    ang_hi = (jnp.arange(n_chunks, dtype=F32) * float(CHUNK))[:, None] * freq_r[None]
    ang_lo = jnp.arange(CHUNK, dtype=F32)[:, None] * freq_r[None]
    dup = lambda t: jnp.concatenate([t, t], axis=-1)
    rhi = jnp.stack([dup(jnp.cos(ang_hi)), dup(jnp.sin(ang_hi))])
    c_lo, s_lo = dup(jnp.cos(ang_lo)), dup(jnp.sin(ang_lo))
    rlo = jnp.stack([c_lo, s_lo, sign * c_lo, sign * s_lo])
    n_pair = ATT_DH // 4
    freq_a = ROPE_THETA ** (-jnp.arange(n_pair, dtype=F32) / n_pair)
    ang_row = jnp.arange(seq // GRID_W, dtype=F32)[:, None] * freq_a[None]
    ang_col = jnp.arange(GRID_W, dtype=F32)[:, None] * freq_a[None]
    in_row = lambda t: jnp.concatenate([t, 0.0 * t, t, 0.0 * t], axis=-1)
    in_col = lambda t: jnp.concatenate([0.0 * t, t, 0.0 * t, t], axis=-1)
    arow = jnp.stack([in_row(jnp.cos(ang_row)), sign * in_row(jnp.sin(ang_row))])
    acol = jnp.stack([in_col(jnp.cos(ang_col)), sign * in_col(jnp.sin(ang_col))])
    return rhi, rlo, arow, acol


def _encode(x, wts, tabs, meta, *, attn_tq):
    batch, seq, _ = x.shape
    proj_meta, ak_meta, vt_meta = meta
    x2 = x.reshape(batch * seq, D_MODEL)
    tm = ATT_KB
    proj, qt, ak, vt = _in_proj(x2, wts["ln1"], wts["w_in"], *tabs, wts["qg"],
                                wts["kg"], tm=tm, table_blocks=seq // tm)
    ret = _retention(proj, proj_meta, wts["lgf"], wts["lgb"], batch=batch, seq=seq)
    att = lax.cond(
        wts["att_bound"][0] <= ATT_BOUND_MAX,
        lambda: _attention_bounded(qt, ak, vt, ak_meta, vt_meta, wts["att_bound"],
                                   batch=batch, seq=seq, tq=attn_tq),
        lambda: _attention_online(qt, ak, vt, ak_meta, vt_meta, batch=batch, seq=seq, tq=attn_tq))
    h1, u2 = _out_proj(x2, ret, att, wts["w_out"], wts["ln2"], tm=512)
    y = _mlp(u2, h1, wts["w_up"], wts["w_down"], wts["fg"], tm=512, tf=1024)
    return y.reshape(batch, seq, D_MODEL)


def kernel(x_prompt, x_sample, meta_tokens, ln1_g, w_in, q_norm_g, k_norm_g, ret_log_decay_fwd,
           ret_log_decay_bwd, w_out, ln2_g, w_up, w_down, final_norm_g):
    assert w_in.shape[0] == 1, "meta-token residual stream is only skippable for a single layer"
    wi = w_in[0]
    w_in_p = jnp.concatenate([
        _deinterleave_cols(wi[:, OFF_RQ:OFF_RV], 2 * RET_HEADS),
        wi[:, OFF_RV:OFF_AQ],
        _deinterleave_cols(wi[:, OFF_AQ:OFF_AV], ATT_HEADS + ATT_KV_HEADS),
        wi[:, OFF_AV:],
    ], axis=1).astype(BF16)
    deint_vec = lambda g: g.reshape(64, 2).T.reshape(1, 128)
    wts = {
        "ln1": ln1_g[0].reshape(1, D_MODEL),
        "w_in": w_in_p,
        "qg": deint_vec(q_norm_g[0]),
        "kg": deint_vec(k_norm_g[0]),
        "lgf": ret_log_decay_fwd[0],
        "lgb": ret_log_decay_bwd[0],
        "w_out": w_out[0].astype(BF16),
        "ln2": ln2_g[0].reshape(1, D_MODEL),
        "w_up": w_up[0].astype(BF16),
        "w_down": w_down[0].astype(BF16),
        "fg": final_norm_g.reshape(1, D_MODEL),
    }
    wts["att_bound"] = (1.02 * LOG2E * ATT_DH ** 0.5 * jnp.max(jnp.abs(q_norm_g[0]))
                        * jnp.max(jnp.abs(k_norm_g[0]))).reshape(1).astype(F32)
    max_seq = max(x_prompt.shape[1], x_sample.shape[1])
    rhi, rlo, arow, acol = _rope_tables(max_seq)

    x_meta = jnp.concatenate(
        [jnp.zeros((CHUNK - N_META, D_MODEL), F32), meta_tokens.astype(F32)], axis=0)
    unrotated = jnp.stack([jnp.ones((CHUNK // GRID_W, 128), F32),
                           jnp.zeros((CHUNK // GRID_W, 128), F32)])
    proj_meta, _, ak_meta, vt_meta = _in_proj(
        x_meta, wts["ln1"], w_in_p, rhi[:, :1], rlo, unrotated, jnp.zeros_like(acol),
        wts["qg"], wts["kg"], tm=CHUNK, table_blocks=1)
    meta = (proj_meta, ak_meta, vt_meta)

    tabs = (rhi[:, 1:], rlo, arow, acol)
    y_prompt = _encode(x_prompt, wts, tabs, meta, attn_tq=1024)
    y_sample = _encode(x_sample, wts, tabs, meta, attn_tq=1024)
    return (y_prompt, y_sample)
```

```python
import jax
import jax.numpy as jnp
from jax import lax
from jax.experimental import pallas as pl
from jax.experimental.pallas import tpu as pltpu

F32 = jnp.float32
BF16 = jnp.bfloat16

D_MODEL = 2048
N_META = 16
GRID_W = 64
CHUNK = 128
RET_HEADS = 4
RET_DK = 128
RET_DV = 256
RET_QK = RET_HEADS * RET_DK
RET_V = RET_HEADS * RET_DV
ATT_DH = 128
ATT_HEADS = 8
ATT_KV_HEADS = 2
ATT_GROUP = ATT_HEADS // ATT_KV_HEADS
ATT_Q = ATT_HEADS * ATT_DH
ATT_KV = ATT_KV_HEADS * ATT_DH
D_FF = 4 * D_MODEL
IN_W = 2 * RET_QK + 2 * RET_V + ATT_Q + 2 * ATT_KV
ROPE_THETA = 10000.0
EPS = 1e-6
LOG2E = 1.4426950408889634

OFF_RQ = 0
OFF_RK = OFF_RQ + RET_QK
OFF_RV = OFF_RK + RET_QK
OFF_RG = OFF_RV + RET_V
OFF_AQ = OFF_RG + RET_V
OFF_AK = OFF_AQ + ATT_Q
OFF_AV = OFF_AK + ATT_KV

PROJ_TN = 512
N_RET_BLOCKS = OFF_AQ // PROJ_TN
IN_PROJ_TM = 2048
IN_PROJ_SUB_ROWS = 256
VMEM_LIMIT = 56 * 1024 * 1024

_NT = (((1,), (1,)), ((), ()))
_TN = (((0,), (0,)), ((), ()))


def _rope_pairs(a, c, s):
    even = lax.broadcasted_iota(jnp.int32, (a.shape[0], 128), 1) % 2 == 0
    outs = []
    for h in range(a.shape[1] // 128):
        ah = a[:, h * 128:(h + 1) * 128]
        partner = jnp.where(even, pltpu.roll(ah, 127, 1), pltpu.roll(ah, 1, 1))
        outs.append(ah * c + partner * s)
    return outs[0] if len(outs) == 1 else jnp.concatenate(outs, axis=1)


def _head_norm(a, g):
    outs = []
    for h in range(a.shape[1] // 128):
        ah = a[:, h * 128:(h + 1) * 128]
        ms = jnp.mean(ah * ah, axis=-1, keepdims=True)
        outs.append(ah * lax.rsqrt(ms + EPS) * g)
    return outs[0] if len(outs) == 1 else jnp.concatenate(outs, axis=1)


def _in_proj_kernel(x_hbm, ln_ref, w_ref, rhi_ref, rlo_ref, arow_ref, acol_ref, qg_ref, kg_ref,
                    o_ref, qt_ref, ak_ref, vt_ref, u_ref, x_ref, x_sem):
    j = pl.program_id(1)
    tm = u_ref.shape[0]
    sub = min(tm, IN_PROJ_SUB_ROWS)

    def ret_tables(rs):
        cs, ss = [], []
        for a in range(rs.start // CHUNK, rs.stop // CHUNK):
            hc, hs = rhi_ref[0, a:a + 1, :], rhi_ref[1, a:a + 1, :]
            cs.append(hc * rlo_ref[0] - hs * rlo_ref[1])
            ss.append(hs * rlo_ref[2] + hc * rlo_ref[3])
        return jnp.concatenate(cs, axis=0), jnp.concatenate(ss, axis=0)

    def axial_tables(rs):
        rows = range(rs.start // GRID_W, rs.stop // GRID_W)
        return (jnp.concatenate([arow_ref[0, g:g + 1, :] + acol_ref[0] for g in rows], axis=0),
                jnp.concatenate([arow_ref[1, g:g + 1, :] + acol_ref[1] for g in rows], axis=0))

    def run(epilogue, normalize=False):
        for r in range(tm // sub):
            rs = slice(r * sub, (r + 1) * sub)
            if normalize:
                x = x_ref[rs, :]
                ms = jnp.mean(x * x, axis=-1, keepdims=True)
                u_ref[rs, :] = (x * lax.rsqrt(ms + EPS) * ln_ref[...]).astype(BF16)
            acc = jnp.dot(u_ref[rs, :], w_ref[...], preferred_element_type=F32)
            epilogue(acc, rs)

    def ret_q(acc, rs):
        o_ref[rs, :] = _rope_pairs(acc, *ret_tables(rs)).astype(BF16)

    def ret_k(acc, rs):
        o_ref[rs, :] = (_rope_pairs(acc, *ret_tables(rs)) * (RET_DK ** -0.5)).astype(BF16)

    def plain(acc, rs):
        o_ref[rs, :] = acc.astype(BF16)

    def att_q(acc, rs):
        qn = _head_norm(acc, qg_ref[...])
        q = _rope_pairs(qn, *axial_tables(rs)) * (ATT_DH ** -0.5 * LOG2E)
        qt_ref[:, rs] = q.T.astype(BF16)

    def att_kv(acc, rs):
        kn = _head_norm(acc[:, :ATT_KV], kg_ref[...])
        ak_ref[rs, :] = _rope_pairs(kn, *axial_tables(rs)).astype(BF16)
        kb = vt_ref.shape[2]
        vt_ref[rs.start // kb, :, rs.start % kb:rs.start % kb + sub] = (
            acc[:, ATT_KV:].T.astype(BF16))

    i = pl.program_id(0)

    def x_copy(tile):
        return pltpu.make_async_copy(x_hbm.at[pl.ds(tile * tm, tm), :], x_ref, x_sem)

    @pl.when(j == OFF_RQ // PROJ_TN)
    def _():
        pl.when(i == 0)(lambda: x_copy(0).start())
        x_copy(i).wait()
        run(ret_q, normalize=True)

    @pl.when(j == OFF_RK // PROJ_TN)
    def _():
        pl.when(i + 1 < pl.num_programs(0))(lambda: x_copy(i + 1).start())
        run(ret_k)

    pl.when((j >= OFF_RV // PROJ_TN) & (j < N_RET_BLOCKS))(lambda: run(plain))
    pl.when((j >= OFF_AQ // PROJ_TN) & (j < OFF_AK // PROJ_TN))(lambda: run(att_q))
    pl.when(j == OFF_AK // PROJ_TN)(lambda: run(att_kv))


def _in_proj(x, ln1, w_in, rhi, rlo, arow, acol, qg, kg, *, tm, table_blocks):
    t = x.shape[0]
    kb = min(tm, ATT_KB)
    n_j = IN_W // PROJ_TN
    j_aq = OFF_AQ // PROJ_TN
    seq_tab = lambda a, n: pl.BlockSpec((a.shape[0], n, 128), lambda i, j: (0, i % table_blocks, 0))
    whole = lambda a: pl.BlockSpec(a.shape, lambda i, j: (0, 0, 0))
    vec = lambda n: pl.BlockSpec((1, n), lambda i, j: (0, 0))
    return pl.pallas_call(
        _in_proj_kernel,
        out_shape=(jax.ShapeDtypeStruct((t, OFF_AQ), BF16),
                   jax.ShapeDtypeStruct((ATT_Q, t), BF16),
                   jax.ShapeDtypeStruct((t, ATT_KV), BF16),
                   jax.ShapeDtypeStruct((t // kb, ATT_KV, kb), BF16)),
        grid=(t // tm, n_j),
        in_specs=[
            pl.BlockSpec(memory_space=pl.ANY),
            vec(D_MODEL),
            pl.BlockSpec((D_MODEL, PROJ_TN), lambda i, j: (0, j)),
            seq_tab(rhi, tm // CHUNK), whole(rlo), seq_tab(arow, tm // GRID_W), whole(acol),
            vec(128), vec(128),
        ],
        out_specs=(
            pl.BlockSpec((tm, PROJ_TN), lambda i, j: (i, jnp.minimum(j, N_RET_BLOCKS - 1))),
            pl.BlockSpec((PROJ_TN, tm), lambda i, j: (jnp.clip(j - j_aq, 0, 1), i)),
            pl.BlockSpec((tm, ATT_KV), lambda i, j: (i, 0)),
            pl.BlockSpec((tm // kb, ATT_KV, kb), lambda i, j: (i, 0, 0)),
        ),
        scratch_shapes=[pltpu.VMEM((tm, D_MODEL), BF16), pltpu.VMEM((tm, D_MODEL), F32),
                        pltpu.SemaphoreType.DMA],
        compiler_params=pltpu.CompilerParams(
            dimension_semantics=("arbitrary", "arbitrary"), vmem_limit_bytes=VMEM_LIMIT),
        name="in_proj",
    )(x, ln1, w_in, rhi, rlo, arow, acol, qg, kg)


RET_STEP_CHUNKS = 4


def _chunk_pos():
    return lax.broadcasted_iota(jnp.int32, (CHUNK, 1), 0).astype(F32)


def _ret_bwd_kernel(lgf_ref, lgb_ref, q_ref, k_ref, v_ref, part_ref, s_ref, d_ref):
    b = pl.program_id(0)
    i = pl.program_id(1)

    @pl.when((b == 0) & (i == 0))
    def _():
        r = lax.broadcasted_iota(jnp.int32, (CHUNK, CHUNK), 0)
        c = lax.broadcasted_iota(jnp.int32, (CHUNK, CHUNK), 1)
        diff = (r - c).astype(F32)
        for h in range(RET_HEADS):
            df = jnp.where(diff >= 0, jnp.exp(lgf_ref[h] * jnp.maximum(diff, 0.0)), 0.0)
            db = jnp.where(diff < 0, jnp.exp(lgb_ref[h] * jnp.maximum(-diff, 0.0)), 0.0)
            d_ref[h] = df + db

    @pl.when(i == 0)
    def _():
        s_ref[...] = jnp.zeros_like(s_ref)

    pos = _chunk_pos()
    n_chunks = q_ref.shape[0] // CHUNK
    for h in range(RET_HEADS):
        lg = lgb_ref[h]
        q_dec = jnp.exp(lg * (float(CHUNK) - pos))
        k_dec = jnp.exp(lg * pos)
        c_dec = jnp.exp(lg * jnp.full((CHUNK, 1), float(CHUNK), F32))
        state = s_ref[h]
        for cc in reversed(range(n_chunks)):
            rs = slice(cc * CHUNK, (cc + 1) * CHUNK)
            q = q_ref[rs, h * RET_DK:(h + 1) * RET_DK]
            k = k_ref[rs, h * RET_DK:(h + 1) * RET_DK]
            v = v_ref[rs, h * RET_DV:(h + 1) * RET_DV]
            sc = lax.dot_general(q, k, _NT, preferred_element_type=F32) * d_ref[h]
            inner = jnp.dot(sc.astype(BF16), v, preferred_element_type=F32)
            cross = jnp.dot(q, state.astype(BF16), preferred_element_type=F32) * q_dec
            part_ref[rs, h * RET_DV:(h + 1) * RET_DV] = inner + cross
            kd = (k.astype(F32) * k_dec).astype(BF16)
            state = state * c_dec + lax.dot_general(kd, v, _TN, preferred_element_type=F32)
        s_ref[h] = state


def _ret_fwd_kernel(lgf_ref, q_ref, k_ref, v_ref, g_ref, km_ref, vm_ref, part_ref,
                    o_ref, s_ref):
    i = pl.program_id(1)
    pos = _chunk_pos()

    @pl.when(i == 0)
    def _():
        for h in range(RET_HEADS):
            k_dec = jnp.exp(lgf_ref[h] * (float(CHUNK - 1) - pos))
            kd = (km_ref[:, h * RET_DK:(h + 1) * RET_DK].astype(F32) * k_dec).astype(BF16)
            s_ref[h] = lax.dot_general(kd, vm_ref[:, h * RET_DV:(h + 1) * RET_DV], _TN,
                                       preferred_element_type=F32)

    n_chunks = q_ref.shape[0] // CHUNK
    for h in range(RET_HEADS):
        lg = lgf_ref[h]
        q_dec = jnp.exp(lg * (pos + 1.0))
        k_dec = jnp.exp(lg * (float(CHUNK - 1) - pos))
        c_dec = jnp.exp(lg * jnp.full((CHUNK, 1), float(CHUNK), F32))
        state = s_ref[h]
        for cc in range(n_chunks):
            rs = slice(cc * CHUNK, (cc + 1) * CHUNK)
            q = q_ref[rs, h * RET_DK:(h + 1) * RET_DK]
            k = k_ref[rs, h * RET_DK:(h + 1) * RET_DK]
            v = v_ref[rs, h * RET_DV:(h + 1) * RET_DV]
            cross = jnp.dot(q, state.astype(BF16), preferred_element_type=F32) * q_dec
            tot = part_ref[rs, h * RET_DV:(h + 1) * RET_DV] + cross
            ms = jnp.mean(tot * tot, axis=-1, keepdims=True)
            g = g_ref[rs, h * RET_DV:(h + 1) * RET_DV].astype(F32)
            gate = g / (1.0 + jnp.exp(-g))
            o_ref[rs, h * RET_DV:(h + 1) * RET_DV] = (
                tot * lax.rsqrt(ms + EPS) * gate).astype(BF16)
            kd = (k.astype(F32) * k_dec).astype(BF16)
            state = state * c_dec + lax.dot_general(kd, v, _TN, preferred_element_type=F32)
        s_ref[h] = state


def _retention(proj, proj_meta, lgf, lgb, *, batch, seq):
    t = proj.shape[0]
    rows_blk = RET_STEP_CHUNKS * CHUNK
    n = seq // rows_blk
    smem = pl.BlockSpec(memory_space=pltpu.SMEM)
    params = pltpu.CompilerParams(dimension_semantics=("arbitrary", "arbitrary"),
                                  vmem_limit_bytes=VMEM_LIMIT)

    def rows_rev(b, i):
        return b * n + (n - 1 - i)

    part = pl.pallas_call(
        _ret_bwd_kernel,
        out_shape=jax.ShapeDtypeStruct((t, RET_V), F32),
        grid=(batch, n),
        in_specs=[
            smem, smem,
            pl.BlockSpec((rows_blk, RET_QK), lambda b, i: (rows_rev(b, i), OFF_RQ // RET_QK)),
            pl.BlockSpec((rows_blk, RET_QK), lambda b, i: (rows_rev(b, i), OFF_RK // RET_QK)),
            pl.BlockSpec((rows_blk, RET_V), lambda b, i: (rows_rev(b, i), OFF_RV // RET_V)),
        ],
        out_specs=pl.BlockSpec((rows_blk, RET_V), lambda b, i: (rows_rev(b, i), 0)),
        scratch_shapes=[pltpu.VMEM((RET_HEADS, RET_DK, RET_DV), F32),
                        pltpu.VMEM((RET_HEADS, CHUNK, CHUNK), F32)],
        compiler_params=params,
        name="retention_bwd",
    )(lgf, lgb, proj, proj, proj)

    def rows(b, i):
        return b * n + i

    return pl.pallas_call(
        _ret_fwd_kernel,
        out_shape=jax.ShapeDtypeStruct((t, RET_V), BF16),
        grid=(batch, n),
        in_specs=[
            smem,
            pl.BlockSpec((rows_blk, RET_QK), lambda b, i: (rows(b, i), OFF_RQ // RET_QK)),
            pl.BlockSpec((rows_blk, RET_QK), lambda b, i: (rows(b, i), OFF_RK // RET_QK)),
            pl.BlockSpec((rows_blk, RET_V), lambda b, i: (rows(b, i), OFF_RV // RET_V)),
            pl.BlockSpec((rows_blk, RET_V), lambda b, i: (rows(b, i), OFF_RG // RET_V)),
            pl.BlockSpec((CHUNK, RET_QK), lambda b, i: (0, OFF_RK // RET_QK)),
            pl.BlockSpec((CHUNK, RET_V), lambda b, i: (0, OFF_RV // RET_V)),
            pl.BlockSpec((rows_blk, RET_V), lambda b, i: (rows(b, i), 0)),
        ],
        out_specs=pl.BlockSpec((rows_blk, RET_V), lambda b, i: (rows(b, i), 0)),
        scratch_shapes=[pltpu.VMEM((RET_HEADS, RET_DK, RET_DV), F32)],
        compiler_params=params,
        name="retention_fwd",
    )(lgf, proj, proj, proj, proj, proj_meta, proj_meta, part)


ATT_SUB = 256
ATT_KB = 1024
ATT_BOUND_MAX = 60.0


def _attn_subtile(k, vt, mask, qt_ref, m_ref, l_ref, acc_ref, g, c):
    qs = slice(c * ATT_SUB, (c + 1) * ATT_SUB)
    st = jnp.dot(k, qt_ref[g * ATT_DH:(g + 1) * ATT_DH, qs], preferred_element_type=F32)
    if mask is not None:
        st = jnp.where(mask, st, -jnp.inf)
    m_prev = m_ref[g, :, qs]
    m_new = jnp.maximum(m_prev, jnp.max(st, axis=0, keepdims=True))
    alpha = jnp.exp2(m_prev - m_new)
    p = jnp.exp2(st - m_new)
    l_ref[g, :, qs] = alpha * l_ref[g, :, qs] + jnp.sum(p, axis=0, keepdims=True)
    acc_ref[g, :, qs] = alpha * acc_ref[g, :, qs] + jnp.dot(
        vt, p.astype(BF16), preferred_element_type=F32)
    m_ref[g, :, qs] = m_new


def _attn_kernel(qt_ref, k_ref, vt_ref, km_ref, vtm_ref, o_ref, m_ref, l_ref, acc_ref):
    ki = pl.program_id(3)
    tq = qt_ref.shape[1]
    tk = k_ref.shape[0]

    @pl.when(ki == 0)
    def _():
        m_ref[...] = jnp.full_like(m_ref, -jnp.inf)
        l_ref[...] = jnp.zeros_like(l_ref)
        acc_ref[...] = jnp.zeros_like(acc_ref)
        is_meta = lax.broadcasted_iota(jnp.int32, (CHUNK, ATT_SUB), 0) >= CHUNK - N_META
        for g in range(ATT_GROUP):
            for c in range(tq // ATT_SUB):
                _attn_subtile(km_ref[...], vtm_ref[...], is_meta, qt_ref, m_ref, l_ref, acc_ref,
                              g, c)

    for r in range(tk // ATT_SUB):
        ks = slice(r * ATT_SUB, (r + 1) * ATT_SUB)
        for g in range(ATT_GROUP):
            for c in range(tq // ATT_SUB):
                _attn_subtile(k_ref[ks, :], vt_ref[:, ks], None, qt_ref, m_ref, l_ref, acc_ref,
                              g, c)

    @pl.when(ki == pl.num_programs(3) - 1)
    def _():
        for g in range(ATT_GROUP):
            o = acc_ref[g] / l_ref[g]
            o_ref[:, g * ATT_DH:(g + 1) * ATT_DH] = o.T.astype(BF16)


def _attn_bounded_kernel(bound_ref, qt_ref, k_ref, vt_ref, km_ref, vtm_ref, o_ref, l_ref, acc_ref):
    tq = qt_ref.shape[1]
    bound = bound_ref[0]
    subs = [slice(c * ATT_SUB, (c + 1) * ATT_SUB) for c in range(tq // ATT_SUB)]

    def update(k, vt, heads, first):
        tiles = [(g, qs) for g in heads for qs in subs]
        sts = [jnp.dot(k, qt_ref[g * ATT_DH:(g + 1) * ATT_DH, qs], preferred_element_type=F32)
               for g, qs in tiles]
        ps = [jnp.exp2(st - bound) for st in sts]
        for (g, qs), p in zip(tiles, ps):
            l = p.reshape(-1, 8, ATT_SUB).sum(axis=0)
            l_ref[g, :, qs] = l if first else l_ref[g, :, qs] + l
        for (g, qs), p in zip(tiles, ps):
            a = jnp.dot(vt, p.astype(BF16), preferred_element_type=F32)
            acc_ref[g, :, qs] = a if first else acc_ref[g, :, qs] + a

    update(km_ref[...], vtm_ref[...], range(ATT_GROUP), True)

    def key_block(kb, carry):
        k = k_ref[pl.ds(pl.multiple_of(kb * ATT_KB, ATT_KB), ATT_KB), :]
        vt = vt_ref[kb]
        update(k, vt, range(ATT_GROUP), False)
        return carry

    lax.fori_loop(0, vt_ref.shape[0], key_block, 0)

    for g in range(ATT_GROUP):
        o = acc_ref[g] / jnp.sum(l_ref[g], axis=0, keepdims=True)
        o_ref[:, g * ATT_DH:(g + 1) * ATT_DH] = o.T.astype(BF16)


def _attention_bounded(qt, ak, vt, ak_meta, vt_meta, bound, *, batch, seq, tq):
    t = ak.shape[0]
    nq = seq // tq
    nkb = seq // ATT_KB
    qw = ATT_GROUP * ATT_DH
    km = ak_meta[CHUNK - N_META:]
    vtm = vt_meta[0][:, CHUNK - N_META:]
    return pl.pallas_call(
        _attn_bounded_kernel,
        out_shape=jax.ShapeDtypeStruct((t, ATT_Q), BF16),
        grid=(batch, ATT_KV_HEADS, nq),
        in_specs=[
            pl.BlockSpec(memory_space=pltpu.SMEM),
            pl.BlockSpec((qw, tq), lambda b, h, qi: (h, b * nq + qi)),
            pl.BlockSpec((seq, ATT_DH), lambda b, h, qi: (b, h)),
            pl.BlockSpec((nkb, ATT_DH, ATT_KB), lambda b, h, qi: (b, h, 0)),
            pl.BlockSpec((N_META, ATT_DH), lambda b, h, qi: (0, h)),
            pl.BlockSpec((ATT_DH, N_META), lambda b, h, qi: (h, 0)),
        ],
        out_specs=pl.BlockSpec((tq, qw), lambda b, h, qi: (b * nq + qi, h)),
        scratch_shapes=[pltpu.VMEM((ATT_GROUP, 8, tq), F32),
                        pltpu.VMEM((ATT_GROUP, ATT_DH, tq), F32)],
        compiler_params=pltpu.CompilerParams(
            dimension_semantics=("arbitrary",) * 3, vmem_limit_bytes=VMEM_LIMIT),
        name="attention_bounded",
    )(bound, qt, ak, vt, km, vtm)


def _attention_online(qt, ak, vt, ak_meta, vt_meta, *, batch, seq, tq):
    t = ak.shape[0]
    nq = seq // tq
    nk = seq // ATT_KB
    qw = ATT_GROUP * ATT_DH
    return pl.pallas_call(
        _attn_kernel,
        out_shape=jax.ShapeDtypeStruct((t, ATT_Q), BF16),
        grid=(batch, ATT_KV_HEADS, nq, nk),
        in_specs=[
            pl.BlockSpec((qw, tq), lambda b, h, qi, ki: (h, b * nq + qi)),
            pl.BlockSpec((ATT_KB, ATT_DH), lambda b, h, qi, ki: (b * nk + ki, h)),
            pl.BlockSpec((None, ATT_DH, ATT_KB), lambda b, h, qi, ki: (b * nk + ki, h, 0)),
            pl.BlockSpec((CHUNK, ATT_DH), lambda b, h, qi, ki: (0, h)),
            pl.BlockSpec((None, ATT_DH, CHUNK), lambda b, h, qi, ki: (0, h, 0)),
        ],
        out_specs=pl.BlockSpec((tq, qw), lambda b, h, qi, ki: (b * nq + qi, h)),
        scratch_shapes=[pltpu.VMEM((ATT_GROUP, 1, tq), F32),
                        pltpu.VMEM((ATT_GROUP, 1, tq), F32),
                        pltpu.VMEM((ATT_GROUP, ATT_DH, tq), F32)],
        compiler_params=pltpu.CompilerParams(
            dimension_semantics=("arbitrary",) * 4, vmem_limit_bytes=VMEM_LIMIT),
        name="attention_online",
    )(qt, ak, vt, ak_meta, vt_meta)


def _out_proj_kernel(x_ref, r_ref, a_ref, wr_ref, wa_ref, ln_ref, h_ref, u_ref):
    h = (x_ref[...]
         + jnp.dot(r_ref[...], wr_ref[...], preferred_element_type=F32)
         + jnp.dot(a_ref[...], wa_ref[...], preferred_element_type=F32))
    h_ref[...] = h
    ms = jnp.mean(h * h, axis=-1, keepdims=True)
    u_ref[...] = (h * lax.rsqrt(ms + EPS) * ln_ref[...]).astype(BF16)


def _out_proj(x, ret, att, w_out, ln2, *, tm):
    t = x.shape[0]
    row = lambda w: pl.BlockSpec((tm, w), lambda i: (i, 0))
    return pl.pallas_call(
        _out_proj_kernel,
        out_shape=(jax.ShapeDtypeStruct((t, D_MODEL), F32),
                   jax.ShapeDtypeStruct((t, D_MODEL), BF16)),
        grid=(t // tm,),
        in_specs=[
            row(D_MODEL), row(RET_V), row(ATT_Q),
            pl.BlockSpec((RET_V, D_MODEL), lambda i: (0, 0)),
            pl.BlockSpec((ATT_Q, D_MODEL), lambda i: (1, 0)),
            pl.BlockSpec((1, D_MODEL), lambda i: (0, 0)),
        ],
        out_specs=(row(D_MODEL), row(D_MODEL)),
        compiler_params=pltpu.CompilerParams(
            dimension_semantics=("arbitrary",), vmem_limit_bytes=VMEM_LIMIT),
        name="out_proj",
    )(x, ret, att, w_out, w_out, ln2)


def _mlp_kernel(u_ref, h_ref, wu_ref, wd_ref, fg_ref, o_ref):
    j = pl.program_id(1)

    @pl.when(j == 0)
    def _():
        o_ref[...] = h_ref[...]

    a = jnp.dot(u_ref[...], wu_ref[...], preferred_element_type=F32)
    a = jnp.square(jnp.maximum(a, 0.0)).astype(BF16)
    o_ref[...] += jnp.dot(a, wd_ref[...], preferred_element_type=F32)

    @pl.when(j == pl.num_programs(1) - 1)
    def _():
        h = o_ref[...]
        ms = jnp.mean(h * h, axis=-1, keepdims=True)
        o_ref[...] = h * lax.rsqrt(ms + EPS) * fg_ref[...]


def _mlp(u, h, w_up, w_down, fg, *, tm, tf):
    t = u.shape[0]
    return pl.pallas_call(
        _mlp_kernel,
        out_shape=jax.ShapeDtypeStruct((t, D_MODEL), F32),
        grid=(t // tm, D_FF // tf),
        in_specs=[
            pl.BlockSpec((tm, D_MODEL), lambda i, j: (i, 0)),
            pl.BlockSpec((tm, D_MODEL), lambda i, j: (i, 0)),
            pl.BlockSpec((D_MODEL, tf), lambda i, j: (0, j)),
            pl.BlockSpec((tf, D_MODEL), lambda i, j: (j, 0)),
            pl.BlockSpec((1, D_MODEL), lambda i, j: (0, 0)),
        ],
        out_specs=pl.BlockSpec((tm, D_MODEL), lambda i, j: (i, 0)),
        compiler_params=pltpu.CompilerParams(
            dimension_semantics=("arbitrary", "arbitrary"), vmem_limit_bytes=VMEM_LIMIT),
        name="mlp",
    )(u, h, w_up, w_down, fg)


def _rope_tables(seq):
    pairs = lambda t: jnp.repeat(t, 2, axis=-1)
    sign = jnp.tile(jnp.array([-1.0, 1.0], F32), 64)
    n_chunks = seq // CHUNK + 1
    freq_r = ROPE_THETA ** (-jnp.linspace(0.0, 1.0, RET_DK // 2, dtype=F32))
    ang_hi = (jnp.arange(n_chunks, dtype=F32) * float(CHUNK))[:, None] * freq_r[None]
    ang_lo = jnp.arange(CHUNK, dtype=F32)[:, None] * freq_r[None]
    rhi = jnp.stack([pairs(jnp.cos(ang_hi)), pairs(jnp.sin(ang_hi))])
    c_lo, s_lo = pairs(jnp.cos(ang_lo)), pairs(jnp.sin(ang_lo))
    rlo = jnp.stack([c_lo, s_lo, sign * c_lo, sign * s_lo])
    n_pair = ATT_DH // 4
    freq_a = ROPE_THETA ** (-jnp.arange(n_pair, dtype=F32) / n_pair)
    ang_row = jnp.arange(seq // GRID_W, dtype=F32)[:, None] * freq_a[None]
    ang_col = jnp.arange(GRID_W, dtype=F32)[:, None] * freq_a[None]
    in_row = lambda t: jnp.concatenate([pairs(t), jnp.zeros_like(pairs(t))], axis=-1)
    in_col = lambda t: jnp.concatenate([jnp.zeros_like(pairs(t)), pairs(t)], axis=-1)
    arow = jnp.stack([in_row(jnp.cos(ang_row)), sign * in_row(jnp.sin(ang_row))])
    acol = jnp.stack([in_col(jnp.cos(ang_col)), sign * in_col(jnp.sin(ang_col))])
    return rhi, rlo, arow, acol


def _encode(x, wts, tabs, meta, *, attn_tq):
    batch, seq, _ = x.shape
    proj_meta, ak_meta, vt_meta = meta
    x2 = x.reshape(batch * seq, D_MODEL)
    tm = IN_PROJ_TM
    proj, qt, ak, vt = _in_proj(x2, wts["ln1"], wts["w_in"], *tabs, wts["qg"],
                                wts["kg"], tm=tm, table_blocks=seq // tm)
    ret = _retention(proj, proj_meta, wts["lgf"], wts["lgb"], batch=batch, seq=seq)
    att = lax.cond(
        wts["att_bound"][0] <= ATT_BOUND_MAX,
        lambda: _attention_bounded(qt, ak, vt, ak_meta, vt_meta, wts["att_bound"],
                                   batch=batch, seq=seq, tq=attn_tq),
        lambda: _attention_online(qt, ak, vt, ak_meta, vt_meta, batch=batch, seq=seq, tq=attn_tq))
    h1, u2 = _out_proj(x2, ret, att, wts["w_out"], wts["ln2"], tm=512)
    y = _mlp(u2, h1, wts["w_up"], wts["w_down"], wts["fg"], tm=512, tf=1024)
    return y.reshape(batch, seq, D_MODEL)


def kernel(x_prompt, x_sample, meta_tokens, ln1_g, w_in, q_norm_g, k_norm_g, ret_log_decay_fwd,
           ret_log_decay_bwd, w_out, ln2_g, w_up, w_down, final_norm_g):
    assert w_in.shape[0] == 1, "meta-token residual stream is only skippable for a single layer"
    wts = {
        "ln1": ln1_g[0].reshape(1, D_MODEL),
        "w_in": w_in[0].astype(BF16),
        "qg": q_norm_g[0].reshape(1, ATT_DH),
        "kg": k_norm_g[0].reshape(1, ATT_DH),
        "lgf": ret_log_decay_fwd[0],
        "lgb": ret_log_decay_bwd[0],
        "w_out": w_out[0].astype(BF16),
        "ln2": ln2_g[0].reshape(1, D_MODEL),
        "w_up": w_up[0].astype(BF16),
        "w_down": w_down[0].astype(BF16),
        "fg": final_norm_g.reshape(1, D_MODEL),
    }
    wts["att_bound"] = (1.02 * LOG2E * ATT_DH ** 0.5 * jnp.max(jnp.abs(q_norm_g[0]))
                        * jnp.max(jnp.abs(k_norm_g[0]))).reshape(1).astype(F32)
    max_seq = max(x_prompt.shape[1], x_sample.shape[1])
    rhi, rlo, arow, acol = _rope_tables(max_seq)

    x_meta = jnp.concatenate(
        [jnp.zeros((CHUNK - N_META, D_MODEL), F32), meta_tokens.astype(F32)], axis=0)
    unrotated = jnp.stack([jnp.ones((CHUNK // GRID_W, 128), F32),
                           jnp.zeros((CHUNK // GRID_W, 128), F32)])
    proj_meta, _, ak_meta, vt_meta = _in_proj(
        x_meta, wts["ln1"], wts["w_in"], rhi[:, :1], rlo, unrotated, jnp.zeros_like(acol),
        wts["qg"], wts["kg"], tm=CHUNK, table_blocks=1)
    meta = (proj_meta, ak_meta, vt_meta)

    tabs = (rhi[:, 1:], rlo, arow, acol)
    y_prompt = _encode(x_prompt, wts, tabs, meta, attn_tq=1024)
    y_sample = _encode(x_sample, wts, tabs, meta, attn_tq=1024)
    return (y_prompt, y_sample)
```

```python
import jax
import jax.numpy as jnp
from jax import lax
from jax.experimental import pallas as pl
from jax.experimental.pallas import tpu as pltpu

F32 = jnp.float32
BF16 = jnp.bfloat16

D_MODEL = 2048
N_META = 16
GRID_W = 64
CHUNK = 128
RET_HEADS = 4
RET_DK = 128
RET_DV = 256
RET_QK = RET_HEADS * RET_DK
RET_V = RET_HEADS * RET_DV
ATT_DH = 128
ATT_HEADS = 8
ATT_KV_HEADS = 2
ATT_GROUP = ATT_HEADS // ATT_KV_HEADS
ATT_Q = ATT_HEADS * ATT_DH
ATT_KV = ATT_KV_HEADS * ATT_DH
D_FF = 4 * D_MODEL
IN_W = 2 * RET_QK + 2 * RET_V + ATT_Q + 2 * ATT_KV
ROPE_THETA = 10000.0
EPS = 1e-6
LOG2E = 1.4426950408889634

OFF_RQ = 0
OFF_RK = OFF_RQ + RET_QK
OFF_RV = OFF_RK + RET_QK
OFF_RG = OFF_RV + RET_V
OFF_AQ = OFF_RG + RET_V
OFF_AK = OFF_AQ + ATT_Q
OFF_AV = OFF_AK + ATT_KV

PROJ_TN = 512
N_RET_BLOCKS = OFF_AQ // PROJ_TN
IN_PROJ_TM = 2048
IN_PROJ_SUB_ROWS = 256
VMEM_LIMIT = 56 * 1024 * 1024

_NT = (((1,), (1,)), ((), ()))
_TN = (((0,), (0,)), ((), ()))


def _rope_pairs(a, c, s):
    even = lax.broadcasted_iota(jnp.int32, (a.shape[0], 128), 1) % 2 == 0
    outs = []
    for h in range(a.shape[1] // 128):
        ah = a[:, h * 128:(h + 1) * 128]
        partner = jnp.where(even, pltpu.roll(ah, 127, 1), pltpu.roll(ah, 1, 1))
        outs.append(ah * c + partner * s)
    return outs[0] if len(outs) == 1 else jnp.concatenate(outs, axis=1)


def _head_norm(a, g):
    outs = []
    for h in range(a.shape[1] // 128):
        ah = a[:, h * 128:(h + 1) * 128]
        ms = jnp.mean(ah * ah, axis=-1, keepdims=True)
        outs.append(ah * lax.rsqrt(ms + EPS) * g)
    return outs[0] if len(outs) == 1 else jnp.concatenate(outs, axis=1)


def _in_proj_kernel(x_hbm, ln_ref, w_ref, rhi_ref, rlo_ref, arow_ref, acol_ref, qg_ref, kg_ref,
                    o_ref, qt_ref, ak_ref, vt_ref, u_ref, x_ref, x_sem):
    j = pl.program_id(1)
    tm = u_ref.shape[0]
    sub = min(tm, IN_PROJ_SUB_ROWS)

    def ret_tables(rs):
        cs, ss = [], []
        for a in range(rs.start // CHUNK, rs.stop // CHUNK):
            hc, hs = rhi_ref[0, a:a + 1, :], rhi_ref[1, a:a + 1, :]
            cs.append(hc * rlo_ref[0] - hs * rlo_ref[1])
            ss.append(hs * rlo_ref[2] + hc * rlo_ref[3])
        return jnp.concatenate(cs, axis=0), jnp.concatenate(ss, axis=0)

    def axial_tables(rs):
        rows = range(rs.start // GRID_W, rs.stop // GRID_W)
        return (jnp.concatenate([arow_ref[0, g:g + 1, :] + acol_ref[0] for g in rows], axis=0),
                jnp.concatenate([arow_ref[1, g:g + 1, :] + acol_ref[1] for g in rows], axis=0))

    def run(epilogue, normalize=False):
        for r in range(tm // sub):
            rs = slice(r * sub, (r + 1) * sub)
            if normalize:
                x = x_ref[rs, :]
                ms = jnp.mean(x * x, axis=-1, keepdims=True)
                u_ref[rs, :] = (x * lax.rsqrt(ms + EPS) * ln_ref[...]).astype(BF16)
            acc = jnp.dot(u_ref[rs, :], w_ref[...], preferred_element_type=F32)
            epilogue(acc, rs)

    def ret_q(acc, rs):
        o_ref[rs, :] = _rope_pairs(acc, *ret_tables(rs)).astype(BF16)

    def ret_k(acc, rs):
        o_ref[rs, :] = (_rope_pairs(acc, *ret_tables(rs)) * (RET_DK ** -0.5)).astype(BF16)

    def plain(acc, rs):
        o_ref[rs, :] = acc.astype(BF16)

    def att_q(acc, rs):
        qn = _head_norm(acc, qg_ref[...])
        q = _rope_pairs(qn, *axial_tables(rs)) * (ATT_DH ** -0.5 * LOG2E)
        qt_ref[:, rs] = q.T.astype(BF16)

    def att_kv(acc, rs):
        kn = _head_norm(acc[:, :ATT_KV], kg_ref[...])
        ak_ref[rs, :] = _rope_pairs(kn, *axial_tables(rs)).astype(BF16)
        kb = vt_ref.shape[2]
        vt_ref[rs.start // kb, :, rs.start % kb:rs.start % kb + sub] = (
            acc[:, ATT_KV:].T.astype(BF16))

    i = pl.program_id(0)

    def x_copy(tile):
        return pltpu.make_async_copy(x_hbm.at[pl.ds(tile * tm, tm), :], x_ref, x_sem)

    @pl.when(j == OFF_RQ // PROJ_TN)
    def _():
        pl.when(i == 0)(lambda: x_copy(0).start())
        x_copy(i).wait()
        run(ret_q, normalize=True)

    @pl.when(j == OFF_RK // PROJ_TN)
    def _():
        pl.when(i + 1 < pl.num_programs(0))(lambda: x_copy(i + 1).start())
        run(ret_k)

    pl.when((j >= OFF_RV // PROJ_TN) & (j < N_RET_BLOCKS))(lambda: run(plain))
    pl.when((j >= OFF_AQ // PROJ_TN) & (j < OFF_AK // PROJ_TN))(lambda: run(att_q))
    pl.when(j == OFF_AK // PROJ_TN)(lambda: run(att_kv))


def _in_proj(x, ln1, w_in, rhi, rlo, arow, acol, qg, kg, *, tm, table_blocks):
    t = x.shape[0]
    kb = min(tm, ATT_KB)
    n_j = IN_W // PROJ_TN
    j_aq = OFF_AQ // PROJ_TN
    seq_tab = lambda a, n: pl.BlockSpec((a.shape[0], n, 128), lambda i, j: (0, i % table_blocks, 0))
    whole = lambda a: pl.BlockSpec(a.shape, lambda i, j: (0, 0, 0))
    vec = lambda n: pl.BlockSpec((1, n), lambda i, j: (0, 0))
    return pl.pallas_call(
        _in_proj_kernel,
        out_shape=(jax.ShapeDtypeStruct((t, OFF_AQ), BF16),
                   jax.ShapeDtypeStruct((ATT_Q, t), BF16),
                   jax.ShapeDtypeStruct((t, ATT_KV), BF16),
                   jax.ShapeDtypeStruct((t // kb, ATT_KV, kb), BF16)),
        grid=(t // tm, n_j),
        in_specs=[
            pl.BlockSpec(memory_space=pl.ANY),
            vec(D_MODEL),
            pl.BlockSpec((D_MODEL, PROJ_TN), lambda i, j: (0, j)),
            seq_tab(rhi, tm // CHUNK), whole(rlo), seq_tab(arow, tm // GRID_W), whole(acol),
            vec(128), vec(128),
        ],
        out_specs=(
            pl.BlockSpec((tm, PROJ_TN), lambda i, j: (i, jnp.minimum(j, N_RET_BLOCKS - 1))),
            pl.BlockSpec((PROJ_TN, tm), lambda i, j: (jnp.clip(j - j_aq, 0, 1), i)),
            pl.BlockSpec((tm, ATT_KV), lambda i, j: (i, 0)),
            pl.BlockSpec((tm // kb, ATT_KV, kb), lambda i, j: (i, 0, 0)),
        ),
        scratch_shapes=[pltpu.VMEM((tm, D_MODEL), BF16), pltpu.VMEM((tm, D_MODEL), F32),
                        pltpu.SemaphoreType.DMA],
        compiler_params=pltpu.CompilerParams(
            dimension_semantics=("arbitrary", "arbitrary"), vmem_limit_bytes=VMEM_LIMIT),
        name="in_proj",
    )(x, ln1, w_in, rhi, rlo, arow, acol, qg, kg)


RET_STEP_CHUNKS = 4


def _chunk_pos():
    return lax.broadcasted_iota(jnp.int32, (CHUNK, 1), 0).astype(F32)


def _ret_bwd_kernel(lgf_ref, lgb_ref, q_ref, k_ref, v_ref, part_ref, s_ref, d_ref):
    b = pl.program_id(0)
    i = pl.program_id(1)

    @pl.when((b == 0) & (i == 0))
    def _():
        r = lax.broadcasted_iota(jnp.int32, (CHUNK, CHUNK), 0)
        c = lax.broadcasted_iota(jnp.int32, (CHUNK, CHUNK), 1)
        diff = (r - c).astype(F32)
        for h in range(RET_HEADS):
            df = jnp.where(diff >= 0, jnp.exp(lgf_ref[h] * jnp.maximum(diff, 0.0)), 0.0)
            db = jnp.where(diff < 0, jnp.exp(lgb_ref[h] * jnp.maximum(-diff, 0.0)), 0.0)
            d_ref[h] = df + db

    @pl.when(i == 0)
    def _():
        s_ref[...] = jnp.zeros_like(s_ref)

    pos = _chunk_pos()
    n_chunks = q_ref.shape[0] // CHUNK
    heads = range(RET_HEADS)
    q_dec = [jnp.exp(lgb_ref[h] * (float(CHUNK) - pos)) for h in heads]
    k_dec = [jnp.exp(lgb_ref[h] * pos) for h in heads]
    c_dec = [jnp.exp(lgb_ref[h] * jnp.full((CHUNK, 1), float(CHUNK), F32)) for h in heads]
    state = [s_ref[h] for h in heads]
    for cc in reversed(range(n_chunks)):
        rs = slice(cc * CHUNK, (cc + 1) * CHUNK)
        q = [q_ref[rs, h * RET_DK:(h + 1) * RET_DK] for h in heads]
        k = [k_ref[rs, h * RET_DK:(h + 1) * RET_DK] for h in heads]
        v = [v_ref[rs, h * RET_DV:(h + 1) * RET_DV] for h in heads]
        sc = [lax.dot_general(q[h], k[h], _NT, preferred_element_type=F32) * d_ref[h]
              for h in heads]
        lhs = [jnp.concatenate([sc[h].astype(BF16),
                                (q[h].astype(F32) * q_dec[h]).astype(BF16)], axis=1)
               for h in heads]
        rhs = [jnp.concatenate([v[h], state[h].astype(BF16)], axis=0) for h in heads]
        for h in heads:
            part_ref[rs, h * RET_DV:(h + 1) * RET_DV] = jnp.dot(
                lhs[h], rhs[h], preferred_element_type=F32)
        kv = [lax.dot_general((k[h].astype(F32) * k_dec[h]).astype(BF16), v[h], _TN,
                              preferred_element_type=F32) for h in heads]
        state = [state[h] * c_dec[h] + kv[h] for h in heads]
    for h in heads:
        s_ref[h] = state[h]


def _ret_fwd_kernel(lgf_ref, q_ref, k_ref, v_ref, g_ref, km_ref, vm_ref, part_ref,
                    o_ref, s_ref):
    i = pl.program_id(1)
    pos = _chunk_pos()

    @pl.when(i == 0)
    def _():
        for h in range(RET_HEADS):
            k_dec = jnp.exp(lgf_ref[h] * (float(CHUNK - 1) - pos))
            kd = (km_ref[:, h * RET_DK:(h + 1) * RET_DK].astype(F32) * k_dec).astype(BF16)
            s_ref[h] = lax.dot_general(kd, vm_ref[:, h * RET_DV:(h + 1) * RET_DV], _TN,
                                       preferred_element_type=F32)

    n_chunks = q_ref.shape[0] // CHUNK
    heads = range(RET_HEADS)
    q_dec = [jnp.exp(lgf_ref[h] * (pos + 1.0)) for h in heads]
    k_dec = [jnp.exp(lgf_ref[h] * (float(CHUNK - 1) - pos)) for h in heads]
    c_dec = [jnp.exp(lgf_ref[h] * jnp.full((CHUNK, 1), float(CHUNK), F32)) for h in heads]
    state = [s_ref[h] for h in heads]
    for cc in range(n_chunks):
        rs = slice(cc * CHUNK, (cc + 1) * CHUNK)
        cols = [slice(h * RET_DV, (h + 1) * RET_DV) for h in heads]
        q = [q_ref[rs, h * RET_DK:(h + 1) * RET_DK] for h in heads]
        k = [k_ref[rs, h * RET_DK:(h + 1) * RET_DK] for h in heads]
        v = [v_ref[rs, cols[h]] for h in heads]
        cross = [jnp.dot(q[h], state[h].astype(BF16), preferred_element_type=F32) * q_dec[h]
                 for h in heads]
        kv = [lax.dot_general((k[h].astype(F32) * k_dec[h]).astype(BF16), v[h], _TN,
                              preferred_element_type=F32) for h in heads]
        for h in heads:
            tot = part_ref[rs, cols[h]] + cross[h]
            ms = jnp.mean(tot * tot, axis=-1, keepdims=True)
            g = g_ref[rs, cols[h]].astype(F32)
            gate = g / (1.0 + jnp.exp(-g))
            o_ref[rs, cols[h]] = (tot * lax.rsqrt(ms + EPS) * gate).astype(BF16)
        state = [state[h] * c_dec[h] + kv[h] for h in heads]
    for h in heads:
        s_ref[h] = state[h]


def _retention(proj, proj_meta, lgf, lgb, *, batch, seq):
    t = proj.shape[0]
    rows_blk = RET_STEP_CHUNKS * CHUNK
    n = seq // rows_blk
    smem = pl.BlockSpec(memory_space=pltpu.SMEM)
    params = pltpu.CompilerParams(dimension_semantics=("arbitrary", "arbitrary"),
                                  vmem_limit_bytes=VMEM_LIMIT)

    def rows_rev(b, i):
        return b * n + (n - 1 - i)

    part = pl.pallas_call(
        _ret_bwd_kernel,
        out_shape=jax.ShapeDtypeStruct((t, RET_V), F32),
        grid=(batch, n),
        in_specs=[
            smem, smem,
            pl.BlockSpec((rows_blk, RET_QK), lambda b, i: (rows_rev(b, i), OFF_RQ // RET_QK)),
            pl.BlockSpec((rows_blk, RET_QK), lambda b, i: (rows_rev(b, i), OFF_RK // RET_QK)),
            pl.BlockSpec((rows_blk, RET_V), lambda b, i: (rows_rev(b, i), OFF_RV // RET_V)),
        ],
        out_specs=pl.BlockSpec((rows_blk, RET_V), lambda b, i: (rows_rev(b, i), 0)),
        scratch_shapes=[pltpu.VMEM((RET_HEADS, RET_DK, RET_DV), F32),
                        pltpu.VMEM((RET_HEADS, CHUNK, CHUNK), F32)],
        compiler_params=params,
        name="retention_bwd",
    )(lgf, lgb, proj, proj, proj)

    def rows(b, i):
        return b * n + i

    return pl.pallas_call(
        _ret_fwd_kernel,
        out_shape=jax.ShapeDtypeStruct((t, RET_V), BF16),
        grid=(batch, n),
        in_specs=[
            smem,
            pl.BlockSpec((rows_blk, RET_QK), lambda b, i: (rows(b, i), OFF_RQ // RET_QK)),
            pl.BlockSpec((rows_blk, RET_QK), lambda b, i: (rows(b, i), OFF_RK // RET_QK)),
            pl.BlockSpec((rows_blk, RET_V), lambda b, i: (rows(b, i), OFF_RV // RET_V)),
            pl.BlockSpec((rows_blk, RET_V), lambda b, i: (rows(b, i), OFF_RG // RET_V)),
            pl.BlockSpec((CHUNK, RET_QK), lambda b, i: (0, OFF_RK // RET_QK)),
            pl.BlockSpec((CHUNK, RET_V), lambda b, i: (0, OFF_RV // RET_V)),
            pl.BlockSpec((rows_blk, RET_V), lambda b, i: (rows(b, i), 0)),
        ],
        out_specs=pl.BlockSpec((rows_blk, RET_V), lambda b, i: (rows(b, i), 0)),
        scratch_shapes=[pltpu.VMEM((RET_HEADS, RET_DK, RET_DV), F32)],
        compiler_params=params,
        name="retention_fwd",
    )(lgf, proj, proj, proj, proj, proj_meta, proj_meta, part)


ATT_SUB = 256
ATT_KB = 1024
ATT_BOUND_MAX = 60.0


def _attn_subtile(k, vt, mask, qt_ref, m_ref, l_ref, acc_ref, g, c):
    qs = slice(c * ATT_SUB, (c + 1) * ATT_SUB)
    st = jnp.dot(k, qt_ref[g * ATT_DH:(g + 1) * ATT_DH, qs], preferred_element_type=F32)
    if mask is not None:
        st = jnp.where(mask, st, -jnp.inf)
    m_prev = m_ref[g, :, qs]
    m_new = jnp.maximum(m_prev, jnp.max(st, axis=0, keepdims=True))
    alpha = jnp.exp2(m_prev - m_new)
    p = jnp.exp2(st - m_new)
    l_ref[g, :, qs] = alpha * l_ref[g, :, qs] + jnp.sum(p, axis=0, keepdims=True)
    acc_ref[g, :, qs] = alpha * acc_ref[g, :, qs] + jnp.dot(
        vt, p.astype(BF16), preferred_element_type=F32)
    m_ref[g, :, qs] = m_new


def _attn_kernel(qt_ref, k_ref, vt_ref, km_ref, vtm_ref, o_ref, m_ref, l_ref, acc_ref):
    ki = pl.program_id(3)
    tq = qt_ref.shape[1]
    tk = k_ref.shape[0]

    @pl.when(ki == 0)
    def _():
        m_ref[...] = jnp.full_like(m_ref, -jnp.inf)
        l_ref[...] = jnp.zeros_like(l_ref)
        acc_ref[...] = jnp.zeros_like(acc_ref)
        is_meta = lax.broadcasted_iota(jnp.int32, (CHUNK, ATT_SUB), 0) >= CHUNK - N_META
        for g in range(ATT_GROUP):
            for c in range(tq // ATT_SUB):
                _attn_subtile(km_ref[...], vtm_ref[...], is_meta, qt_ref, m_ref, l_ref, acc_ref,
                              g, c)

    for r in range(tk // ATT_SUB):
        ks = slice(r * ATT_SUB, (r + 1) * ATT_SUB)
        for g in range(ATT_GROUP):
            for c in range(tq // ATT_SUB):
                _attn_subtile(k_ref[ks, :], vt_ref[:, ks], None, qt_ref, m_ref, l_ref, acc_ref,
                              g, c)

    @pl.when(ki == pl.num_programs(3) - 1)
    def _():
        for g in range(ATT_GROUP):
            o = acc_ref[g] / l_ref[g]
            o_ref[:, g * ATT_DH:(g + 1) * ATT_DH] = o.T.astype(BF16)


def _attn_bounded_kernel(bound_ref, qt_ref, k_ref, vt_ref, km_ref, vtm_ref, o_ref, l_ref, acc_ref):
    tq = qt_ref.shape[1]
    bound = bound_ref[0]
    subs = [slice(c * ATT_SUB, (c + 1) * ATT_SUB) for c in range(tq // ATT_SUB)]

    def update(k, vt, heads, first):
        tiles = [(g, qs) for g in heads for qs in subs]
        sts = [jnp.dot(k, qt_ref[g * ATT_DH:(g + 1) * ATT_DH, qs], preferred_element_type=F32)
               for g, qs in tiles]
        ps = [jnp.exp2(st - bound) for st in sts]
        for (g, qs), p in zip(tiles, ps):
            l = p.reshape(-1, 8, ATT_SUB).sum(axis=0)
            l_ref[g, :, qs] = l if first else l_ref[g, :, qs] + l
        for (g, qs), p in zip(tiles, ps):
            a = jnp.dot(vt, p.astype(BF16), preferred_element_type=F32)
            acc_ref[g, :, qs] = a if first else acc_ref[g, :, qs] + a

    update(km_ref[...], vtm_ref[...], range(ATT_GROUP), True)

    def key_block(kb, carry):
        k = k_ref[pl.ds(pl.multiple_of(kb * ATT_KB, ATT_KB), ATT_KB), :]
        vt = vt_ref[kb]
        update(k, vt, range(ATT_GROUP), False)
        return carry

    lax.fori_loop(0, vt_ref.shape[0], key_block, 0)

    for g in range(ATT_GROUP):
        o = acc_ref[g] / jnp.sum(l_ref[g], axis=0, keepdims=True)
        o_ref[:, g * ATT_DH:(g + 1) * ATT_DH] = o.T.astype(BF16)


def _attention_bounded(qt, ak, vt, ak_meta, vt_meta, bound, *, batch, seq, tq):
    t = ak.shape[0]
    nq = seq // tq
    nkb = seq // ATT_KB
    qw = ATT_GROUP * ATT_DH
    km = ak_meta[CHUNK - N_META:]
    vtm = vt_meta[0][:, CHUNK - N_META:]
    return pl.pallas_call(
        _attn_bounded_kernel,
        out_shape=jax.ShapeDtypeStruct((t, ATT_Q), BF16),
        grid=(batch, ATT_KV_HEADS, nq),
        in_specs=[
            pl.BlockSpec(memory_space=pltpu.SMEM),
            pl.BlockSpec((qw, tq), lambda b, h, qi: (h, b * nq + qi)),
            pl.BlockSpec((seq, ATT_DH), lambda b, h, qi: (b, h)),
            pl.BlockSpec((nkb, ATT_DH, ATT_KB), lambda b, h, qi: (b, h, 0)),
            pl.BlockSpec((N_META, ATT_DH), lambda b, h, qi: (0, h)),
            pl.BlockSpec((ATT_DH, N_META), lambda b, h, qi: (h, 0)),
        ],
        out_specs=pl.BlockSpec((tq, qw), lambda b, h, qi: (b * nq + qi, h)),
        scratch_shapes=[pltpu.VMEM((ATT_GROUP, 8, tq), F32),
                        pltpu.VMEM((ATT_GROUP, ATT_DH, tq), F32)],
        compiler_params=pltpu.CompilerParams(
            dimension_semantics=("arbitrary",) * 3, vmem_limit_bytes=VMEM_LIMIT),
        name="attention_bounded",
    )(bound, qt, ak, vt, km, vtm)


def _attention_online(qt, ak, vt, ak_meta, vt_meta, *, batch, seq, tq):
    t = ak.shape[0]
    nq = seq // tq
    nk = seq // ATT_KB
    qw = ATT_GROUP * ATT_DH
    return pl.pallas_call(
        _attn_kernel,
        out_shape=jax.ShapeDtypeStruct((t, ATT_Q), BF16),
        grid=(batch, ATT_KV_HEADS, nq, nk),
        in_specs=[
            pl.BlockSpec((qw, tq), lambda b, h, qi, ki: (h, b * nq + qi)),
            pl.BlockSpec((ATT_KB, ATT_DH), lambda b, h, qi, ki: (b * nk + ki, h)),
            pl.BlockSpec((None, ATT_DH, ATT_KB), lambda b, h, qi, ki: (b * nk + ki, h, 0)),
            pl.BlockSpec((CHUNK, ATT_DH), lambda b, h, qi, ki: (0, h)),
            pl.BlockSpec((None, ATT_DH, CHUNK), lambda b, h, qi, ki: (0, h, 0)),
        ],
        out_specs=pl.BlockSpec((tq, qw), lambda b, h, qi, ki: (b * nq + qi, h)),
        scratch_shapes=[pltpu.VMEM((ATT_GROUP, 1, tq), F32),
                        pltpu.VMEM((ATT_GROUP, 1, tq), F32),
                        pltpu.VMEM((ATT_GROUP, ATT_DH, tq), F32)],
        compiler_params=pltpu.CompilerParams(
            dimension_semantics=("arbitrary",) * 4, vmem_limit_bytes=VMEM_LIMIT),
        name="attention_online",
    )(qt, ak, vt, ak_meta, vt_meta)


def _out_proj_kernel(x_ref, r_ref, a_ref, wr_ref, wa_ref, ln_ref, h_ref, u_ref):
    h = (x_ref[...]
         + jnp.dot(r_ref[...], wr_ref[...], preferred_element_type=F32)
         + jnp.dot(a_ref[...], wa_ref[...], preferred_element_type=F32))
    h_ref[...] = h
    ms = jnp.mean(h * h, axis=-1, keepdims=True)
    u_ref[...] = (h * lax.rsqrt(ms + EPS) * ln_ref[...]).astype(BF16)


def _out_proj(x, ret, att, w_out, ln2, *, tm):
    t = x.shape[0]
    row = lambda w: pl.BlockSpec((tm, w), lambda i: (i, 0))
    return pl.pallas_call(
        _out_proj_kernel,
        out_shape=(jax.ShapeDtypeStruct((t, D_MODEL), F32),
                   jax.ShapeDtypeStruct((t, D_MODEL), BF16)),
        grid=(t // tm,),
        in_specs=[
            row(D_MODEL), row(RET_V), row(ATT_Q),
            pl.BlockSpec((RET_V, D_MODEL), lambda i: (0, 0)),
            pl.BlockSpec((ATT_Q, D_MODEL), lambda i: (1, 0)),
            pl.BlockSpec((1, D_MODEL), lambda i: (0, 0)),
        ],
        out_specs=(row(D_MODEL), row(D_MODEL)),
        compiler_params=pltpu.CompilerParams(
            dimension_semantics=("arbitrary",), vmem_limit_bytes=VMEM_LIMIT),
        name="out_proj",
    )(x, ret, att, w_out, w_out, ln2)


def _mlp_kernel(u_ref, h_ref, wu_ref, wd_ref, fg_ref, o_ref):
    j = pl.program_id(1)

    @pl.when(j == 0)
    def _():
        o_ref[...] = h_ref[...]

    a = jnp.dot(u_ref[...], wu_ref[...], preferred_element_type=F32)
    a = jnp.square(jnp.maximum(a, 0.0)).astype(BF16)
    o_ref[...] += jnp.dot(a, wd_ref[...], preferred_element_type=F32)

    @pl.when(j == pl.num_programs(1) - 1)
    def _():
        h = o_ref[...]
        ms = jnp.mean(h * h, axis=-1, keepdims=True)
        o_ref[...] = h * lax.rsqrt(ms + EPS) * fg_ref[...]


def _mlp(u, h, w_up, w_down, fg, *, tm, tf):
    t = u.shape[0]
    return pl.pallas_call(
        _mlp_kernel,
        out_shape=jax.ShapeDtypeStruct((t, D_MODEL), F32),
        grid=(t // tm, D_FF // tf),
        in_specs=[
            pl.BlockSpec((tm, D_MODEL), lambda i, j: (i, 0)),
            pl.BlockSpec((tm, D_MODEL), lambda i, j: (i, 0)),
            pl.BlockSpec((D_MODEL, tf), lambda i, j: (0, j)),
            pl.BlockSpec((tf, D_MODEL), lambda i, j: (j, 0)),
            pl.BlockSpec((1, D_MODEL), lambda i, j: (0, 0)),
        ],
        out_specs=pl.BlockSpec((tm, D_MODEL), lambda i, j: (i, 0)),
        compiler_params=pltpu.CompilerParams(
            dimension_semantics=("arbitrary", "arbitrary"), vmem_limit_bytes=VMEM_LIMIT),
        name="mlp",
    )(u, h, w_up, w_down, fg)


def _rope_tables(seq):
    pairs = lambda t: jnp.repeat(t, 2, axis=-1)
    sign = jnp.tile(jnp.array([-1.0, 1.0], F32), 64)
    n_chunks = seq // CHUNK + 1
    freq_r = ROPE_THETA ** (-jnp.linspace(0.0, 1.0, RET_DK // 2, dtype=F32))
    ang_hi = (jnp.arange(n_chunks, dtype=F32) * float(CHUNK))[:, None] * freq_r[None]
    ang_lo = jnp.arange(CHUNK, dtype=F32)[:, None] * freq_r[None]
    rhi = jnp.stack([pairs(jnp.cos(ang_hi)), pairs(jnp.sin(ang_hi))])
    c_lo, s_lo = pairs(jnp.cos(ang_lo)), pairs(jnp.sin(ang_lo))
    rlo = jnp.stack([c_lo, s_lo, sign * c_lo, sign * s_lo])
    n_pair = ATT_DH // 4
    freq_a = ROPE_THETA ** (-jnp.arange(n_pair, dtype=F32) / n_pair)
    ang_row = jnp.arange(seq // GRID_W, dtype=F32)[:, None] * freq_a[None]
    ang_col = jnp.arange(GRID_W, dtype=F32)[:, None] * freq_a[None]
    in_row = lambda t: jnp.concatenate([pairs(t), jnp.zeros_like(pairs(t))], axis=-1)
    in_col = lambda t: jnp.concatenate([jnp.zeros_like(pairs(t)), pairs(t)], axis=-1)
    arow = jnp.stack([in_row(jnp.cos(ang_row)), sign * in_row(jnp.sin(ang_row))])
    acol = jnp.stack([in_col(jnp.cos(ang_col)), sign * in_col(jnp.sin(ang_col))])
    return rhi, rlo, arow, acol


def _encode(x, wts, tabs, meta, *, attn_tq):
    batch, seq, _ = x.shape
    proj_meta, ak_meta, vt_meta = meta
    x2 = x.reshape(batch * seq, D_MODEL)
    tm = IN_PROJ_TM
    proj, qt, ak, vt = _in_proj(x2, wts["ln1"], wts["w_in"], *tabs, wts["qg"],
                                wts["kg"], tm=tm, table_blocks=seq // tm)
    ret = _retention(proj, proj_meta, wts["lgf"], wts["lgb"], batch=batch, seq=seq)
    att = lax.cond(
        wts["att_bound"][0] <= ATT_BOUND_MAX,
        lambda: _attention_bounded(qt, ak, vt, ak_meta, vt_meta, wts["att_bound"],
                                   batch=batch, seq=seq, tq=attn_tq),
        lambda: _attention_online(qt, ak, vt, ak_meta, vt_meta, batch=batch, seq=seq, tq=attn_tq))
    h1, u2 = _out_proj(x2, ret, att, wts["w_out"], wts["ln2"], tm=512)
    y = _mlp(u2, h1, wts["w_up"], wts["w_down"], wts["fg"], tm=512, tf=2048)
    return y.reshape(batch, seq, D_MODEL)


def kernel(x_prompt, x_sample, meta_tokens, ln1_g, w_in, q_norm_g, k_norm_g, ret_log_decay_fwd,
           ret_log_decay_bwd, w_out, ln2_g, w_up, w_down, final_norm_g):
    assert w_in.shape[0] == 1, "meta-token residual stream is only skippable for a single layer"
    wts = {
        "ln1": ln1_g[0].reshape(1, D_MODEL),
        "w_in": w_in[0].astype(BF16),
        "qg": q_norm_g[0].reshape(1, ATT_DH),
        "kg": k_norm_g[0].reshape(1, ATT_DH),
        "lgf": ret_log_decay_fwd[0],
        "lgb": ret_log_decay_bwd[0],
        "w_out": w_out[0].astype(BF16),
        "ln2": ln2_g[0].reshape(1, D_MODEL),
        "w_up": w_up[0].astype(BF16),
        "w_down": w_down[0].astype(BF16),
        "fg": final_norm_g.reshape(1, D_MODEL),
    }
    wts["att_bound"] = (1.02 * LOG2E * ATT_DH ** 0.5 * jnp.max(jnp.abs(q_norm_g[0]))
                        * jnp.max(jnp.abs(k_norm_g[0]))).reshape(1).astype(F32)
    max_seq = max(x_prompt.shape[1], x_sample.shape[1])
    rhi, rlo, arow, acol = _rope_tables(max_seq)

    x_meta = jnp.concatenate(
        [jnp.zeros((CHUNK - N_META, D_MODEL), F32), meta_tokens.astype(F32)], axis=0)
    unrotated = jnp.stack([jnp.ones((CHUNK // GRID_W, 128), F32),
                           jnp.zeros((CHUNK // GRID_W, 128), F32)])
    proj_meta, _, ak_meta, vt_meta = _in_proj(
        x_meta, wts["ln1"], wts["w_in"], rhi[:, :1], rlo, unrotated, jnp.zeros_like(acol),
        wts["qg"], wts["kg"], tm=CHUNK, table_blocks=1)
    meta = (proj_meta, ak_meta, vt_meta)

    tabs = (rhi[:, 1:], rlo, arow, acol)
    y_prompt = _encode(x_prompt, wts, tabs, meta, attn_tq=1024)
    y_sample = _encode(x_sample, wts, tabs, meta, attn_tq=1024)
    return (y_prompt, y_sample)
```

```python
import jax
import jax.numpy as jnp
from jax import lax
from jax.experimental import pallas as pl
from jax.experimental.pallas import tpu as pltpu

F32 = jnp.float32
BF16 = jnp.bfloat16

D_MODEL = 2048
N_META = 16
GRID_W = 64
CHUNK = 128
RET_HEADS = 4
RET_DK = 128
RET_DV = 256
RET_QK = RET_HEADS * RET_DK
RET_V = RET_HEADS * RET_DV
ATT_DH = 128
ATT_HEADS = 8
ATT_KV_HEADS = 2
ATT_GROUP = ATT_HEADS // ATT_KV_HEADS
ATT_Q = ATT_HEADS * ATT_DH
ATT_KV = ATT_KV_HEADS * ATT_DH
D_FF = 4 * D_MODEL
IN_W = 2 * RET_QK + 2 * RET_V + ATT_Q + 2 * ATT_KV
ROPE_THETA = 10000.0
EPS = 1e-6
LOG2E = 1.4426950408889634

OFF_RQ = 0
OFF_RK = OFF_RQ + RET_QK
OFF_RV = OFF_RK + RET_QK
OFF_RG = OFF_RV + RET_V
OFF_AQ = OFF_RG + RET_V
OFF_AK = OFF_AQ + ATT_Q
OFF_AV = OFF_AK + ATT_KV

LANES = 128
V7X_VMEM_BYTES = 64 * 1024 * 1024
VMEM_LIMIT = V7X_VMEM_BYTES * 7 // 8

PROJ_TN = 512
N_RET_BLOCKS = OFF_AQ // PROJ_TN
IN_PROJ_TM = 2048
IN_PROJ_SUB_ROWS = 256
RET_STEP_CHUNKS = 8
ATT_TQ = 1024
ATT_SUB = 256
ATT_KB = 2048
OUT_PROJ_TM = 512
MLP_TM = 512
MLP_TF = 2048
ATT_BOUND_MAX = 60.0

_NT = (((1,), (1,)), ((), ()))
_TN = (((0,), (0,)), ((), ()))


def _rope_pairs(a, c, s):
    even = lax.broadcasted_iota(jnp.int32, (a.shape[0], LANES), 1) % 2 == 0
    outs = []
    for h in range(a.shape[1] // LANES):
        ah = a[:, h * LANES:(h + 1) * LANES]
        partner = jnp.where(even, pltpu.roll(ah, LANES - 1, 1), pltpu.roll(ah, 1, 1))
        outs.append(ah * c + partner * s)
    return outs[0] if len(outs) == 1 else jnp.concatenate(outs, axis=1)


def _head_norm(a, g):
    outs = []
    for h in range(a.shape[1] // LANES):
        ah = a[:, h * LANES:(h + 1) * LANES]
        ms = jnp.mean(ah * ah, axis=-1, keepdims=True)
        outs.append(ah * lax.rsqrt(ms + EPS) * g)
    return outs[0] if len(outs) == 1 else jnp.concatenate(outs, axis=1)


def _in_proj_kernel(x_hbm, ln_ref, w_ref, rhi_ref, rlo_ref, arow_ref, acol_ref, qg_ref, kg_ref,
                    o_ref, qt_ref, ak_ref, vt_ref, u_ref, x_ref, x_sem):
    j = pl.program_id(1)
    tm = u_ref.shape[0]
    sub = min(tm, IN_PROJ_SUB_ROWS)

    def ret_tables(rs):
        cs, ss = [], []
        for a in range(rs.start // CHUNK, rs.stop // CHUNK):
            hc, hs = rhi_ref[0, a:a + 1, :], rhi_ref[1, a:a + 1, :]
            cs.append(hc * rlo_ref[0] - hs * rlo_ref[1])
            ss.append(hs * rlo_ref[2] + hc * rlo_ref[3])
        return jnp.concatenate(cs, axis=0), jnp.concatenate(ss, axis=0)

    def axial_tables(rs):
        rows = range(rs.start // GRID_W, rs.stop // GRID_W)
        return (jnp.concatenate([arow_ref[0, g:g + 1, :] + acol_ref[0] for g in rows], axis=0),
                jnp.concatenate([arow_ref[1, g:g + 1, :] + acol_ref[1] for g in rows], axis=0))

    def run(epilogue, normalize=False):
        for r in range(tm // sub):
            rs = slice(r * sub, (r + 1) * sub)
            if normalize:
                x = x_ref[rs, :]
                ms = jnp.mean(x * x, axis=-1, keepdims=True)
                u_ref[rs, :] = (x * lax.rsqrt(ms + EPS) * ln_ref[...]).astype(BF16)
            acc = jnp.dot(u_ref[rs, :], w_ref[...], preferred_element_type=F32)
            epilogue(acc, rs)

    def ret_q(acc, rs):
        o_ref[rs, :] = _rope_pairs(acc, *ret_tables(rs)).astype(BF16)

    def ret_k(acc, rs):
        o_ref[rs, :] = (_rope_pairs(acc, *ret_tables(rs)) * (RET_DK ** -0.5)).astype(BF16)

    def plain(acc, rs):
        o_ref[rs, :] = acc.astype(BF16)

    def att_q(acc, rs):
        qn = _head_norm(acc, qg_ref[...])
        q = _rope_pairs(qn, *axial_tables(rs)) * (ATT_DH ** -0.5 * LOG2E)
        qt_ref[:, rs] = q.T.astype(BF16)

    def att_kv(acc, rs):
        kn = _head_norm(acc[:, :ATT_KV], kg_ref[...])
        ak_ref[rs, :] = _rope_pairs(kn, *axial_tables(rs)).astype(BF16)
        kb = vt_ref.shape[2]
        vt_ref[rs.start // kb, :, rs.start % kb:rs.start % kb + sub] = (
            acc[:, ATT_KV:].T.astype(BF16))

    i = pl.program_id(0)

    def x_copy(tile):
        return pltpu.make_async_copy(x_hbm.at[pl.ds(tile * tm, tm), :], x_ref, x_sem)

    @pl.when(j == OFF_RQ // PROJ_TN)
    def _():
        pl.when(i == 0)(lambda: x_copy(0).start())
        x_copy(i).wait()
        run(ret_q, normalize=True)

    @pl.when(j == OFF_RK // PROJ_TN)
    def _():
        pl.when(i + 1 < pl.num_programs(0))(lambda: x_copy(i + 1).start())
        run(ret_k)

    pl.when((j >= OFF_RV // PROJ_TN) & (j < N_RET_BLOCKS))(lambda: run(plain))
    pl.when((j >= OFF_AQ // PROJ_TN) & (j < OFF_AK // PROJ_TN))(lambda: run(att_q))
    pl.when(j == OFF_AK // PROJ_TN)(lambda: run(att_kv))


def _in_proj(x, ln1, w_in, rhi, rlo, arow, acol, qg, kg, *, tm, table_blocks):
    t = x.shape[0]
    kb = min(tm, ATT_KB)
    n_j = IN_W // PROJ_TN
    j_aq = OFF_AQ // PROJ_TN
    seq_tab = lambda a, n: pl.BlockSpec((a.shape[0], n, LANES), lambda i, j: (0, i % table_blocks, 0))
    whole = lambda a: pl.BlockSpec(a.shape, lambda i, j: (0, 0, 0))
    vec = lambda n: pl.BlockSpec((1, n), lambda i, j: (0, 0))
    return pl.pallas_call(
        _in_proj_kernel,
        out_shape=(jax.ShapeDtypeStruct((t, OFF_AQ), BF16),
                   jax.ShapeDtypeStruct((ATT_Q, t), BF16),
                   jax.ShapeDtypeStruct((t, ATT_KV), BF16),
                   jax.ShapeDtypeStruct((t // kb, ATT_KV, kb), BF16)),
        grid=(t // tm, n_j),
        in_specs=[
            pl.BlockSpec(memory_space=pl.ANY),
            vec(D_MODEL),
            pl.BlockSpec((D_MODEL, PROJ_TN), lambda i, j: (0, j)),
            seq_tab(rhi, tm // CHUNK), whole(rlo), seq_tab(arow, tm // GRID_W), whole(acol),
            vec(ATT_DH), vec(ATT_DH),
        ],
        out_specs=(
            pl.BlockSpec((tm, PROJ_TN), lambda i, j: (i, jnp.minimum(j, N_RET_BLOCKS - 1))),
            pl.BlockSpec((PROJ_TN, tm), lambda i, j: (jnp.clip(j - j_aq, 0, 1), i)),
            pl.BlockSpec((tm, ATT_KV), lambda i, j: (i, 0)),
            pl.BlockSpec((tm // kb, ATT_KV, kb), lambda i, j: (i, 0, 0)),
        ),
        scratch_shapes=[pltpu.VMEM((tm, D_MODEL), BF16), pltpu.VMEM((tm, D_MODEL), F32),
                        pltpu.SemaphoreType.DMA],
        compiler_params=pltpu.CompilerParams(
            dimension_semantics=("arbitrary", "arbitrary"), vmem_limit_bytes=VMEM_LIMIT),
        name="in_proj",
    )(x, ln1, w_in, rhi, rlo, arow, acol, qg, kg)


def _chunk_pos():
    return lax.broadcasted_iota(jnp.int32, (CHUNK, 1), 0).astype(F32)


def _ret_bwd_kernel(lgf_ref, lgb_ref, q_ref, k_ref, v_ref, part_ref, s_ref, d_ref):
    b = pl.program_id(0)
    i = pl.program_id(1)

    @pl.when((b == 0) & (i == 0))
    def _():
        r = lax.broadcasted_iota(jnp.int32, (CHUNK, CHUNK), 0)
        c = lax.broadcasted_iota(jnp.int32, (CHUNK, CHUNK), 1)
        diff = (r - c).astype(F32)
        for h in range(RET_HEADS):
            df = jnp.where(diff >= 0, jnp.exp(lgf_ref[h] * jnp.maximum(diff, 0.0)), 0.0)
            db = jnp.where(diff < 0, jnp.exp(lgb_ref[h] * jnp.maximum(-diff, 0.0)), 0.0)
            d_ref[h] = df + db

    @pl.when(i == 0)
    def _():
        s_ref[...] = jnp.zeros_like(s_ref)

    pos = _chunk_pos()
    n_chunks = q_ref.shape[0] // CHUNK
    heads = range(RET_HEADS)
    q_dec = [jnp.exp(lgb_ref[h] * (float(CHUNK) - pos)) for h in heads]
    k_dec = [jnp.exp(lgb_ref[h] * pos) for h in heads]
    c_dec = [jnp.exp(lgb_ref[h] * jnp.full((CHUNK, 1), float(CHUNK), F32)) for h in heads]
    state = [s_ref[h] for h in heads]
    for cc in reversed(range(n_chunks)):
        rs = slice(cc * CHUNK, (cc + 1) * CHUNK)
        q = [q_ref[rs, h * RET_DK:(h + 1) * RET_DK] for h in heads]
        k = [k_ref[rs, h * RET_DK:(h + 1) * RET_DK] for h in heads]
        v = [v_ref[rs, h * RET_DV:(h + 1) * RET_DV] for h in heads]
        sc = [lax.dot_general(q[h], k[h], _NT, preferred_element_type=F32) * d_ref[h]
              for h in heads]
        lhs = [jnp.concatenate([sc[h].astype(BF16),
                                (q[h].astype(F32) * q_dec[h]).astype(BF16)], axis=1)
               for h in heads]
        rhs = [jnp.concatenate([v[h], state[h].astype(BF16)], axis=0) for h in heads]
        for h in heads:
            part_ref[rs, h * RET_DV:(h + 1) * RET_DV] = jnp.dot(
                lhs[h], rhs[h], preferred_element_type=F32)
        kv = [lax.dot_general((k[h].astype(F32) * k_dec[h]).astype(BF16), v[h], _TN,
                              preferred_element_type=F32) for h in heads]
        state = [state[h] * c_dec[h] + kv[h] for h in heads]
    for h in heads:
        s_ref[h] = state[h]


def _ret_fwd_kernel(lgf_ref, q_ref, k_ref, v_ref, g_ref, km_ref, vm_ref, part_ref,
                    o_ref, s_ref):
    i = pl.program_id(1)
    pos = _chunk_pos()

    @pl.when(i == 0)
    def _():
        for h in range(RET_HEADS):
            k_dec = jnp.exp(lgf_ref[h] * (float(CHUNK - 1) - pos))
            kd = (km_ref[:, h * RET_DK:(h + 1) * RET_DK].astype(F32) * k_dec).astype(BF16)
            s_ref[h] = lax.dot_general(kd, vm_ref[:, h * RET_DV:(h + 1) * RET_DV], _TN,
                                       preferred_element_type=F32)

    n_chunks = q_ref.shape[0] // CHUNK
    heads = range(RET_HEADS)
    q_dec = [jnp.exp(lgf_ref[h] * (pos + 1.0)) for h in heads]
    k_dec = [jnp.exp(lgf_ref[h] * (float(CHUNK - 1) - pos)) for h in heads]
    c_dec = [jnp.exp(lgf_ref[h] * jnp.full((CHUNK, 1), float(CHUNK), F32)) for h in heads]
    state = [s_ref[h] for h in heads]
    for cc in range(n_chunks):
        rs = slice(cc * CHUNK, (cc + 1) * CHUNK)
        cols = [slice(h * RET_DV, (h + 1) * RET_DV) for h in heads]
        q = [q_ref[rs, h * RET_DK:(h + 1) * RET_DK] for h in heads]
        k = [k_ref[rs, h * RET_DK:(h + 1) * RET_DK] for h in heads]
        v = [v_ref[rs, cols[h]] for h in heads]
        cross = [jnp.dot(q[h], state[h].astype(BF16), preferred_element_type=F32) * q_dec[h]
                 for h in heads]
        kv = [lax.dot_general((k[h].astype(F32) * k_dec[h]).astype(BF16), v[h], _TN,
                              preferred_element_type=F32) for h in heads]
        for h in heads:
            tot = part_ref[rs, cols[h]] + cross[h]
            ms = jnp.mean(tot * tot, axis=-1, keepdims=True)
            g = g_ref[rs, cols[h]].astype(F32)
            gate = g / (1.0 + jnp.exp(-g))
            o_ref[rs, cols[h]] = (tot * lax.rsqrt(ms + EPS) * gate).astype(BF16)
        state = [state[h] * c_dec[h] + kv[h] for h in heads]
    for h in heads:
        s_ref[h] = state[h]


def _retention(proj, proj_meta, lgf, lgb, *, batch, seq):
    t = proj.shape[0]
    rows_blk = RET_STEP_CHUNKS * CHUNK
    n = seq // rows_blk
    smem = pl.BlockSpec(memory_space=pltpu.SMEM)
    params = pltpu.CompilerParams(dimension_semantics=("arbitrary", "arbitrary"),
                                  vmem_limit_bytes=VMEM_LIMIT)

    def rows_rev(b, i):
        return b * n + (n - 1 - i)

    part = pl.pallas_call(
        _ret_bwd_kernel,
        out_shape=jax.ShapeDtypeStruct((t, RET_V), F32),
        grid=(batch, n),
        in_specs=[
            smem, smem,
            pl.BlockSpec((rows_blk, RET_QK), lambda b, i: (rows_rev(b, i), OFF_RQ // RET_QK)),
            pl.BlockSpec((rows_blk, RET_QK), lambda b, i: (rows_rev(b, i), OFF_RK // RET_QK)),
            pl.BlockSpec((rows_blk, RET_V), lambda b, i: (rows_rev(b, i), OFF_RV // RET_V)),
        ],
        out_specs=pl.BlockSpec((rows_blk, RET_V), lambda b, i: (rows_rev(b, i), 0)),
        scratch_shapes=[pltpu.VMEM((RET_HEADS, RET_DK, RET_DV), F32),
                        pltpu.VMEM((RET_HEADS, CHUNK, CHUNK), F32)],
        compiler_params=params,
        name="retention_bwd",
    )(lgf, lgb, proj, proj, proj)

    def rows(b, i):
        return b * n + i

    return pl.pallas_call(
        _ret_fwd_kernel,
        out_shape=jax.ShapeDtypeStruct((t, RET_V), BF16),
        grid=(batch, n),
        in_specs=[
            smem,
            pl.BlockSpec((rows_blk, RET_QK), lambda b, i: (rows(b, i), OFF_RQ // RET_QK)),
            pl.BlockSpec((rows_blk, RET_QK), lambda b, i: (rows(b, i), OFF_RK // RET_QK)),
            pl.BlockSpec((rows_blk, RET_V), lambda b, i: (rows(b, i), OFF_RV // RET_V)),
            pl.BlockSpec((rows_blk, RET_V), lambda b, i: (rows(b, i), OFF_RG // RET_V)),
            pl.BlockSpec((CHUNK, RET_QK), lambda b, i: (0, OFF_RK // RET_QK)),
            pl.BlockSpec((CHUNK, RET_V), lambda b, i: (0, OFF_RV // RET_V)),
            pl.BlockSpec((rows_blk, RET_V), lambda b, i: (rows(b, i), 0)),
        ],
        out_specs=pl.BlockSpec((rows_blk, RET_V), lambda b, i: (rows(b, i), 0)),
        scratch_shapes=[pltpu.VMEM((RET_HEADS, RET_DK, RET_DV), F32)],
        compiler_params=params,
        name="retention_fwd",
    )(lgf, proj, proj, proj, proj, proj_meta, proj_meta, part)


def _attn_subtile(k, vt, mask, qt_ref, m_ref, l_ref, acc_ref, g, c):
    qs = slice(c * ATT_SUB, (c + 1) * ATT_SUB)
    st = jnp.dot(k, qt_ref[g * ATT_DH:(g + 1) * ATT_DH, qs], preferred_element_type=F32)
    if mask is not None:
        st = jnp.where(mask, st, -jnp.inf)
    m_prev = m_ref[g, :, qs]
    m_new = jnp.maximum(m_prev, jnp.max(st, axis=0, keepdims=True))
    alpha = jnp.exp2(m_prev - m_new)
    p = jnp.exp2(st - m_new)
    l_ref[g, :, qs] = alpha * l_ref[g, :, qs] + jnp.sum(p, axis=0, keepdims=True)
    acc_ref[g, :, qs] = alpha * acc_ref[g, :, qs] + jnp.dot(
        vt, p.astype(BF16), preferred_element_type=F32)
    m_ref[g, :, qs] = m_new


def _attn_kernel(qt_ref, k_ref, vt_ref, km_ref, vtm_ref, o_ref, m_ref, l_ref, acc_ref):
    ki = pl.program_id(3)
    tq = qt_ref.shape[1]
    tk = k_ref.shape[0]

    @pl.when(ki == 0)
    def _():
        m_ref[...] = jnp.full_like(m_ref, -jnp.inf)
        l_ref[...] = jnp.zeros_like(l_ref)
        acc_ref[...] = jnp.zeros_like(acc_ref)
        is_meta = lax.broadcasted_iota(jnp.int32, (CHUNK, ATT_SUB), 0) >= CHUNK - N_META
        for g in range(ATT_GROUP):
            for c in range(tq // ATT_SUB):
                _attn_subtile(km_ref[...], vtm_ref[...], is_meta, qt_ref, m_ref, l_ref, acc_ref,
                              g, c)

    for r in range(tk // ATT_SUB):
        ks = slice(r * ATT_SUB, (r + 1) * ATT_SUB)
        for g in range(ATT_GROUP):
            for c in range(tq // ATT_SUB):
                _attn_subtile(k_ref[ks, :], vt_ref[:, ks], None, qt_ref, m_ref, l_ref, acc_ref,
                              g, c)

    @pl.when(ki == pl.num_programs(3) - 1)
    def _():
        for g in range(ATT_GROUP):
            o = acc_ref[g] / l_ref[g]
            o_ref[:, g * ATT_DH:(g + 1) * ATT_DH] = o.T.astype(BF16)


def _attn_bounded_kernel(bound_ref, qt_ref, k_ref, vt_ref, km_ref, vtm_ref, o_ref, l_ref, acc_ref):
    tq = qt_ref.shape[1]
    bound = bound_ref[0]
    subs = [slice(c * ATT_SUB, (c + 1) * ATT_SUB) for c in range(tq // ATT_SUB)]

    def update(k, vt, heads, first):
        tiles = [(g, qs) for g in heads for qs in subs]
        sts = [jnp.dot(k, qt_ref[g * ATT_DH:(g + 1) * ATT_DH, qs], preferred_element_type=F32)
               for g, qs in tiles]
        ps = [jnp.exp2(st - bound) for st in sts]
        for (g, qs), p in zip(tiles, ps):
            l = p.reshape(-1, 8, ATT_SUB).sum(axis=0)
            l_ref[g, :, qs] = l if first else l_ref[g, :, qs] + l
        for (g, qs), p in zip(tiles, ps):
            a = jnp.dot(vt, p.astype(BF16), preferred_element_type=F32)
            acc_ref[g, :, qs] = a if first else acc_ref[g, :, qs] + a

    update(km_ref[...], vtm_ref[...], range(ATT_GROUP), True)

    def key_block(kb, carry):
        k = k_ref[pl.ds(pl.multiple_of(kb * ATT_KB, ATT_KB), ATT_KB), :]
        vt = vt_ref[kb]
        update(k, vt, range(ATT_GROUP), False)
        return carry

    lax.fori_loop(0, vt_ref.shape[0], key_block, 0)

    for g in range(ATT_GROUP):
        o = acc_ref[g] / jnp.sum(l_ref[g], axis=0, keepdims=True)
        o_ref[:, g * ATT_DH:(g + 1) * ATT_DH] = o.T.astype(BF16)


def _attention_bounded(qt, ak, vt, ak_meta, vt_meta, bound, *, batch, seq, tq):
    t = ak.shape[0]
    nq = seq // tq
    nkb = seq // ATT_KB
    qw = ATT_GROUP * ATT_DH
    km = ak_meta[CHUNK - N_META:]
    vtm = vt_meta[0][:, CHUNK - N_META:]
    return pl.pallas_call(
        _attn_bounded_kernel,
        out_shape=jax.ShapeDtypeStruct((t, ATT_Q), BF16),
        grid=(batch, ATT_KV_HEADS, nq),
        in_specs=[
            pl.BlockSpec(memory_space=pltpu.SMEM),
            pl.BlockSpec((qw, tq), lambda b, h, qi: (h, b * nq + qi)),
            pl.BlockSpec((seq, ATT_DH), lambda b, h, qi: (b, h)),
            pl.BlockSpec((nkb, ATT_DH, ATT_KB), lambda b, h, qi: (b, h, 0)),
            pl.BlockSpec((N_META, ATT_DH), lambda b, h, qi: (0, h)),
            pl.BlockSpec((ATT_DH, N_META), lambda b, h, qi: (h, 0)),
        ],
        out_specs=pl.BlockSpec((tq, qw), lambda b, h, qi: (b * nq + qi, h)),
        scratch_shapes=[pltpu.VMEM((ATT_GROUP, 8, tq), F32),
                        pltpu.VMEM((ATT_GROUP, ATT_DH, tq), F32)],
        compiler_params=pltpu.CompilerParams(
            dimension_semantics=("arbitrary",) * 3, vmem_limit_bytes=VMEM_LIMIT),
        name="attention_bounded",
    )(bound, qt, ak, vt, km, vtm)


def _attention_online(qt, ak, vt, ak_meta, vt_meta, *, batch, seq, tq):
    t = ak.shape[0]
    nq = seq // tq
    nk = seq // ATT_KB
    qw = ATT_GROUP * ATT_DH
    return pl.pallas_call(
        _attn_kernel,
        out_shape=jax.ShapeDtypeStruct((t, ATT_Q), BF16),
        grid=(batch, ATT_KV_HEADS, nq, nk),
        in_specs=[
            pl.BlockSpec((qw, tq), lambda b, h, qi, ki: (h, b * nq + qi)),
            pl.BlockSpec((ATT_KB, ATT_DH), lambda b, h, qi, ki: (b * nk + ki, h)),
            pl.BlockSpec((None, ATT_DH, ATT_KB), lambda b, h, qi, ki: (b * nk + ki, h, 0)),
            pl.BlockSpec((CHUNK, ATT_DH), lambda b, h, qi, ki: (0, h)),
            pl.BlockSpec((None, ATT_DH, CHUNK), lambda b, h, qi, ki: (0, h, 0)),
        ],
        out_specs=pl.BlockSpec((tq, qw), lambda b, h, qi, ki: (b * nq + qi, h)),
        scratch_shapes=[pltpu.VMEM((ATT_GROUP, 1, tq), F32),
                        pltpu.VMEM((ATT_GROUP, 1, tq), F32),
                        pltpu.VMEM((ATT_GROUP, ATT_DH, tq), F32)],
        compiler_params=pltpu.CompilerParams(
            dimension_semantics=("arbitrary",) * 4, vmem_limit_bytes=VMEM_LIMIT),
        name="attention_online",
    )(qt, ak, vt, ak_meta, vt_meta)


def _out_proj_kernel(x_ref, r_ref, a_ref, wr_ref, wa_ref, ln_ref, h_ref, u_ref):
    h = (x_ref[...]
         + jnp.dot(r_ref[...], wr_ref[...], preferred_element_type=F32)
         + jnp.dot(a_ref[...], wa_ref[...], preferred_element_type=F32))
    h_ref[...] = h
    ms = jnp.mean(h * h, axis=-1, keepdims=True)
    u_ref[...] = (h * lax.rsqrt(ms + EPS) * ln_ref[...]).astype(BF16)


def _out_proj(x, ret, att, w_out, ln2, *, tm):
    t = x.shape[0]
    row = lambda w: pl.BlockSpec((tm, w), lambda i: (i, 0))
    return pl.pallas_call(
        _out_proj_kernel,
        out_shape=(jax.ShapeDtypeStruct((t, D_MODEL), F32),
                   jax.ShapeDtypeStruct((t, D_MODEL), BF16)),
        grid=(t // tm,),
        in_specs=[
            row(D_MODEL), row(RET_V), row(ATT_Q),
            pl.BlockSpec((RET_V, D_MODEL), lambda i: (0, 0)),
            pl.BlockSpec((ATT_Q, D_MODEL), lambda i: (1, 0)),
            pl.BlockSpec((1, D_MODEL), lambda i: (0, 0)),
        ],
        out_specs=(row(D_MODEL), row(D_MODEL)),
        compiler_params=pltpu.CompilerParams(
            dimension_semantics=("arbitrary",), vmem_limit_bytes=VMEM_LIMIT),
        name="out_proj",
    )(x, ret, att, w_out, w_out, ln2)


def _mlp_kernel(u_ref, h_ref, wu_ref, wd_ref, fg_ref, o_ref):
    j = pl.program_id(1)

    @pl.when(j == 0)
    def _():
        o_ref[...] = h_ref[...]

    a = jnp.dot(u_ref[...], wu_ref[...], preferred_element_type=F32)
    a = jnp.square(jnp.maximum(a, 0.0)).astype(BF16)
    o_ref[...] += jnp.dot(a, wd_ref[...], preferred_element_type=F32)

    @pl.when(j == pl.num_programs(1) - 1)
    def _():
        h = o_ref[...]
        ms = jnp.mean(h * h, axis=-1, keepdims=True)
        o_ref[...] = h * lax.rsqrt(ms + EPS) * fg_ref[...]


def _mlp(u, h, w_up, w_down, fg, *, tm, tf):
    t = u.shape[0]
    return pl.pallas_call(
        _mlp_kernel,
        out_shape=jax.ShapeDtypeStruct((t, D_MODEL), F32),
        grid=(t // tm, D_FF // tf),
        in_specs=[
            pl.BlockSpec((tm, D_MODEL), lambda i, j: (i, 0)),
            pl.BlockSpec((tm, D_MODEL), lambda i, j: (i, 0)),
            pl.BlockSpec((D_MODEL, tf), lambda i, j: (0, j)),
            pl.BlockSpec((tf, D_MODEL), lambda i, j: (j, 0)),
            pl.BlockSpec((1, D_MODEL), lambda i, j: (0, 0)),
        ],
        out_specs=pl.BlockSpec((tm, D_MODEL), lambda i, j: (i, 0)),
        compiler_params=pltpu.CompilerParams(
            dimension_semantics=("arbitrary", "arbitrary"), vmem_limit_bytes=VMEM_LIMIT),
        name="mlp",
    )(u, h, w_up, w_down, fg)


def _rope_tables(seq):
    pairs = lambda t: jnp.repeat(t, 2, axis=-1)
    sign = jnp.tile(jnp.array([-1.0, 1.0], F32), 64)
    n_chunks = seq // CHUNK + 1
    freq_r = ROPE_THETA ** (-jnp.linspace(0.0, 1.0, RET_DK // 2, dtype=F32))
    ang_hi = (jnp.arange(n_chunks, dtype=F32) * float(CHUNK))[:, None] * freq_r[None]
    ang_lo = jnp.arange(CHUNK, dtype=F32)[:, None] * freq_r[None]
    rhi = jnp.stack([pairs(jnp.cos(ang_hi)), pairs(jnp.sin(ang_hi))])
    c_lo, s_lo = pairs(jnp.cos(ang_lo)), pairs(jnp.sin(ang_lo))
    rlo = jnp.stack([c_lo, s_lo, sign * c_lo, sign * s_lo])
    n_pair = ATT_DH // 4
    freq_a = ROPE_THETA ** (-jnp.arange(n_pair, dtype=F32) / n_pair)
    ang_row = jnp.arange(seq // GRID_W, dtype=F32)[:, None] * freq_a[None]
    ang_col = jnp.arange(GRID_W, dtype=F32)[:, None] * freq_a[None]
    in_row = lambda t: jnp.concatenate([pairs(t), jnp.zeros_like(pairs(t))], axis=-1)
    in_col = lambda t: jnp.concatenate([jnp.zeros_like(pairs(t)), pairs(t)], axis=-1)
    arow = jnp.stack([in_row(jnp.cos(ang_row)), sign * in_row(jnp.sin(ang_row))])
    acol = jnp.stack([in_col(jnp.cos(ang_col)), sign * in_col(jnp.sin(ang_col))])
    return rhi, rlo, arow, acol


def _encode(x, wts, tabs, meta):
    batch, seq, _ = x.shape
    for tile in (IN_PROJ_TM, RET_STEP_CHUNKS * CHUNK, ATT_TQ, ATT_KB, OUT_PROJ_TM, MLP_TM):
        assert seq % tile == 0, (seq, tile)
    proj_meta, ak_meta, vt_meta = meta
    x2 = x.reshape(batch * seq, D_MODEL)
    proj, qt, ak, vt = _in_proj(x2, wts["ln1"], wts["w_in"], *tabs, wts["qg"], wts["kg"],
                                tm=IN_PROJ_TM, table_blocks=seq // IN_PROJ_TM)
    ret = _retention(proj, proj_meta, wts["lgf"], wts["lgb"], batch=batch, seq=seq)
    att = lax.cond(
        wts["att_bound"][0] <= ATT_BOUND_MAX,
        lambda: _attention_bounded(qt, ak, vt, ak_meta, vt_meta, wts["att_bound"],
                                   batch=batch, seq=seq, tq=ATT_TQ),
        lambda: _attention_online(qt, ak, vt, ak_meta, vt_meta, batch=batch, seq=seq, tq=ATT_TQ))
    h1, u2 = _out_proj(x2, ret, att, wts["w_out"], wts["ln2"], tm=OUT_PROJ_TM)
    y = _mlp(u2, h1, wts["w_up"], wts["w_down"], wts["fg"], tm=MLP_TM, tf=MLP_TF)
    return y.reshape(batch, seq, D_MODEL)


def kernel(x_prompt, x_sample, meta_tokens, ln1_g, w_in, q_norm_g, k_norm_g, ret_log_decay_fwd,
           ret_log_decay_bwd, w_out, ln2_g, w_up, w_down, final_norm_g):
    assert w_in.shape[0] == 1, "meta-token residual stream is only skippable for a single layer"
    wts = {
        "ln1": ln1_g[0].reshape(1, D_MODEL),
        "w_in": w_in[0].astype(BF16),
        "qg": q_norm_g[0].reshape(1, ATT_DH),
        "kg": k_norm_g[0].reshape(1, ATT_DH),
        "lgf": ret_log_decay_fwd[0],
        "lgb": ret_log_decay_bwd[0],
        "w_out": w_out[0].astype(BF16),
        "ln2": ln2_g[0].reshape(1, D_MODEL),
        "w_up": w_up[0].astype(BF16),
        "w_down": w_down[0].astype(BF16),
        "fg": final_norm_g.reshape(1, D_MODEL),
    }
    wts["att_bound"] = (1.02 * LOG2E * ATT_DH ** 0.5 * jnp.max(jnp.abs(q_norm_g[0]))
                        * jnp.max(jnp.abs(k_norm_g[0]))).reshape(1).astype(F32)
    max_seq = max(x_prompt.shape[1], x_sample.shape[1])
    rhi, rlo, arow, acol = _rope_tables(max_seq)

    x_meta = jnp.concatenate(
        [jnp.zeros((CHUNK - N_META, D_MODEL), F32), meta_tokens.astype(F32)], axis=0)
    unrotated = jnp.stack([jnp.ones((CHUNK // GRID_W, LANES), F32),
                           jnp.zeros((CHUNK // GRID_W, LANES), F32)])
    proj_meta, _, ak_meta, vt_meta = _in_proj(
        x_meta, wts["ln1"], wts["w_in"], rhi[:, :1], rlo, unrotated, jnp.zeros_like(acol),
        wts["qg"], wts["kg"], tm=CHUNK, table_blocks=1)
    meta = (proj_meta, ak_meta, vt_meta)

    tabs = (rhi[:, 1:], rlo, arow, acol)
    y_prompt = _encode(x_prompt, wts, tabs, meta)
    y_sample = _encode(x_sample, wts, tabs, meta)
    return (y_prompt, y_sample)
```

```python
import jax
import jax.numpy as jnp
from jax import lax
from jax.experimental import pallas as pl
from jax.experimental.pallas import tpu as pltpu

F32 = jnp.float32
BF16 = jnp.bfloat16

D_MODEL = 2048
N_META = 16
GRID_W = 64
CHUNK = 128
RET_HEADS = 4
RET_DK = 128
RET_DV = 256
RET_QK = RET_HEADS * RET_DK
RET_V = RET_HEADS * RET_DV
ATT_DH = 128
ATT_HEADS = 8
ATT_KV_HEADS = 2
ATT_GROUP = ATT_HEADS // ATT_KV_HEADS
ATT_Q = ATT_HEADS * ATT_DH
ATT_KV = ATT_KV_HEADS * ATT_DH
D_FF = 4 * D_MODEL
IN_W = 2 * RET_QK + 2 * RET_V + ATT_Q + 2 * ATT_KV
ROPE_THETA = 10000.0
EPS = 1e-6
LOG2E = 1.4426950408889634

OFF_RQ = 0
OFF_RK = OFF_RQ + RET_QK
OFF_RV = OFF_RK + RET_QK
OFF_RG = OFF_RV + RET_V
OFF_AQ = OFF_RG + RET_V
OFF_AK = OFF_AQ + ATT_Q
OFF_AV = OFF_AK + ATT_KV

LANES = 128
V7X_VMEM_BYTES = 64 * 1024 * 1024
VMEM_LIMIT = V7X_VMEM_BYTES * 7 // 8

PROJ_TN = 512
N_RET_BLOCKS = OFF_AQ // PROJ_TN
IN_PROJ_TM = 2048
IN_PROJ_SUB_ROWS = 256
RET_STEP_CHUNKS = 8
ATT_TQ = 1024
ATT_SUB = 256
ATT_KB = 2048
OUT_PROJ_TM = 512
MLP_TM = 512
MLP_TF = 2048
ATT_BOUND_MAX = 60.0

_NT = (((1,), (1,)), ((), ()))
_TN = (((0,), (0,)), ((), ()))


def _rope_pairs(a, c, s):
    even = lax.broadcasted_iota(jnp.int32, (a.shape[0], LANES), 1) % 2 == 0
    outs = []
    for h in range(a.shape[1] // LANES):
        ah = a[:, h * LANES:(h + 1) * LANES]
        partner = jnp.where(even, pltpu.roll(ah, LANES - 1, 1), pltpu.roll(ah, 1, 1))
        outs.append(ah * c + partner * s)
    return outs[0] if len(outs) == 1 else jnp.concatenate(outs, axis=1)


def _head_norm(a, g):
    outs = []
    for h in range(a.shape[1] // LANES):
        ah = a[:, h * LANES:(h + 1) * LANES]
        ms = jnp.mean(ah * ah, axis=-1, keepdims=True)
        outs.append(ah * lax.rsqrt(ms + EPS) * g)
    return outs[0] if len(outs) == 1 else jnp.concatenate(outs, axis=1)


def _in_proj_kernel(x_hbm, ln_ref, w_ref, rhi_ref, rlo_ref, arow_ref, acol_ref, qg_ref, kg_ref,
                    o_ref, qt_ref, ak_ref, vt_ref, u_ref, x_ref, x_sem):
    j = pl.program_id(1)
    tm = u_ref.shape[0]
    sub = min(tm, IN_PROJ_SUB_ROWS)

    def ret_tables(rs):
        cs, ss = [], []
        for a in range(rs.start // CHUNK, rs.stop // CHUNK):
            hc, hs = rhi_ref[0, a:a + 1, :], rhi_ref[1, a:a + 1, :]
            cs.append(hc * rlo_ref[0] - hs * rlo_ref[1])
            ss.append(hs * rlo_ref[2] + hc * rlo_ref[3])
        return jnp.concatenate(cs, axis=0), jnp.concatenate(ss, axis=0)

    def axial_tables(rs):
        rows = range(rs.start // GRID_W, rs.stop // GRID_W)
        return (jnp.concatenate([arow_ref[0, g:g + 1, :] + acol_ref[0] for g in rows], axis=0),
                jnp.concatenate([arow_ref[1, g:g + 1, :] + acol_ref[1] for g in rows], axis=0))

    def run(epilogue, normalize=False):
        for r in range(tm // sub):
            rs = slice(r * sub, (r + 1) * sub)
            if normalize:
                x = x_ref[rs, :]
                ms = jnp.mean(x * x, axis=-1, keepdims=True)
                u_ref[rs, :] = (x * lax.rsqrt(ms + EPS) * ln_ref[...]).astype(BF16)
            acc = jnp.dot(u_ref[rs, :], w_ref[...], preferred_element_type=F32)
            epilogue(acc, rs)

    def ret_q(acc, rs):
        o_ref[rs, :] = _rope_pairs(acc, *ret_tables(rs)).astype(BF16)

    def ret_k(acc, rs):
        o_ref[rs, :] = (_rope_pairs(acc, *ret_tables(rs)) * (RET_DK ** -0.5)).astype(BF16)

    def plain(acc, rs):
        o_ref[rs, :] = acc.astype(BF16)

    def silu_gate(acc, rs):
        o_ref[rs, :] = (acc / (1.0 + jnp.exp(-acc))).astype(BF16)

    def att_q(acc, rs):
        qn = _head_norm(acc, qg_ref[...])
        q = _rope_pairs(qn, *axial_tables(rs)) * (ATT_DH ** -0.5 * LOG2E)
        qt_ref[:, rs] = q.T.astype(BF16)

    def att_kv(acc, rs):
        kn = _head_norm(acc[:, :ATT_KV], kg_ref[...])
        ak_ref[rs, :] = _rope_pairs(kn, *axial_tables(rs)).astype(BF16)
        kb = vt_ref.shape[2]
        vt_ref[rs.start // kb, :, rs.start % kb:rs.start % kb + sub] = (
            acc[:, ATT_KV:].T.astype(BF16))

    i = pl.program_id(0)

    def x_copy(tile):
        return pltpu.make_async_copy(x_hbm.at[pl.ds(tile * tm, tm), :], x_ref, x_sem)

    @pl.when(j == OFF_RQ // PROJ_TN)
    def _():
        pl.when(i == 0)(lambda: x_copy(0).start())
        x_copy(i).wait()
        run(ret_q, normalize=True)

    @pl.when(j == OFF_RK // PROJ_TN)
    def _():
        pl.when(i + 1 < pl.num_programs(0))(lambda: x_copy(i + 1).start())
        run(ret_k)

    pl.when((j >= OFF_RV // PROJ_TN) & (j < OFF_RG // PROJ_TN))(lambda: run(plain))
    pl.when((j >= OFF_RG // PROJ_TN) & (j < N_RET_BLOCKS))(lambda: run(silu_gate))
    pl.when((j >= OFF_AQ // PROJ_TN) & (j < OFF_AK // PROJ_TN))(lambda: run(att_q))
    pl.when(j == OFF_AK // PROJ_TN)(lambda: run(att_kv))


def _in_proj(x, ln1, w_in, rhi, rlo, arow, acol, qg, kg, *, tm, table_blocks):
    t = x.shape[0]
    kb = min(tm, ATT_KB)
    n_j = IN_W // PROJ_TN
    j_aq = OFF_AQ // PROJ_TN
    seq_tab = lambda a, n: pl.BlockSpec((a.shape[0], n, LANES), lambda i, j: (0, i % table_blocks, 0))
    whole = lambda a: pl.BlockSpec(a.shape, lambda i, j: (0, 0, 0))
    vec = lambda n: pl.BlockSpec((1, n), lambda i, j: (0, 0))
    return pl.pallas_call(
        _in_proj_kernel,
        out_shape=(jax.ShapeDtypeStruct((t, OFF_AQ), BF16),
                   jax.ShapeDtypeStruct((ATT_Q, t), BF16),
                   jax.ShapeDtypeStruct((t, ATT_KV), BF16),
                   jax.ShapeDtypeStruct((t // kb, ATT_KV, kb), BF16)),
        grid=(t // tm, n_j),
        in_specs=[
            pl.BlockSpec(memory_space=pl.ANY),
            vec(D_MODEL),
            pl.BlockSpec((D_MODEL, PROJ_TN), lambda i, j: (0, j)),
            seq_tab(rhi, tm // CHUNK), whole(rlo), seq_tab(arow, tm // GRID_W), whole(acol),
            vec(ATT_DH), vec(ATT_DH),
        ],
        out_specs=(
            pl.BlockSpec((tm, PROJ_TN), lambda i, j: (i, jnp.minimum(j, N_RET_BLOCKS - 1))),
            pl.BlockSpec((PROJ_TN, tm), lambda i, j: (jnp.clip(j - j_aq, 0, 1), i)),
            pl.BlockSpec((tm, ATT_KV), lambda i, j: (i, 0)),
            pl.BlockSpec((tm // kb, ATT_KV, kb), lambda i, j: (i, 0, 0)),
        ),
        scratch_shapes=[pltpu.VMEM((tm, D_MODEL), BF16), pltpu.VMEM((tm, D_MODEL), F32),
                        pltpu.SemaphoreType.DMA],
        compiler_params=pltpu.CompilerParams(
            dimension_semantics=("arbitrary", "arbitrary"), vmem_limit_bytes=VMEM_LIMIT),
        name="in_proj",
    )(x, ln1, w_in, rhi, rlo, arow, acol, qg, kg)


def _chunk_pos():
    return lax.broadcasted_iota(jnp.int32, (CHUNK, 1), 0).astype(F32)


def _ret_bwd_kernel(lgf_ref, lgb_ref, q_ref, k_ref, v_ref, part_ref, s_ref, d_ref):
    b = pl.program_id(0)
    i = pl.program_id(1)

    @pl.when((b == 0) & (i == 0))
    def _():
        r = lax.broadcasted_iota(jnp.int32, (CHUNK, CHUNK), 0)
        c = lax.broadcasted_iota(jnp.int32, (CHUNK, CHUNK), 1)
        diff = (r - c).astype(F32)
        for h in range(RET_HEADS):
            df = jnp.where(diff >= 0, jnp.exp(lgf_ref[h] * jnp.maximum(diff, 0.0)), 0.0)
            db = jnp.where(diff < 0, jnp.exp(lgb_ref[h] * jnp.maximum(-diff, 0.0)), 0.0)
            d_ref[h] = df + db

    @pl.when(i == 0)
    def _():
        s_ref[...] = jnp.zeros_like(s_ref)

    pos = _chunk_pos()
    n_chunks = q_ref.shape[0] // CHUNK
    heads = range(RET_HEADS)
    q_dec = [jnp.exp(lgb_ref[h] * (float(CHUNK) - pos)) for h in heads]
    k_dec = [jnp.exp(lgb_ref[h] * pos) for h in heads]
    c_dec = [jnp.exp(lgb_ref[h] * jnp.full((CHUNK, 1), float(CHUNK), F32)) for h in heads]
    state = [s_ref[h] for h in heads]
    for cc in reversed(range(n_chunks)):
        rs = slice(cc * CHUNK, (cc + 1) * CHUNK)
        q = [q_ref[rs, h * RET_DK:(h + 1) * RET_DK] for h in heads]
        k = [k_ref[rs, h * RET_DK:(h + 1) * RET_DK] for h in heads]
        v = [v_ref[rs, h * RET_DV:(h + 1) * RET_DV] for h in heads]
        sc = [lax.dot_general(q[h], k[h], _NT, preferred_element_type=F32) * d_ref[h]
              for h in heads]
        lhs = [jnp.concatenate([sc[h].astype(BF16),
                                (q[h].astype(F32) * q_dec[h]).astype(BF16)], axis=1)
               for h in heads]
        rhs = [jnp.concatenate([v[h], state[h].astype(BF16)], axis=0) for h in heads]
        for h in heads:
            part_ref[rs, h * RET_DV:(h + 1) * RET_DV] = jnp.dot(
                lhs[h], rhs[h], preferred_element_type=F32)
        kv = [lax.dot_general((k[h].astype(F32) * k_dec[h]).astype(BF16), v[h], _TN,
                              preferred_element_type=F32) for h in heads]
        state = [state[h] * c_dec[h] + kv[h] for h in heads]
    for h in heads:
        s_ref[h] = state[h]


def _ret_fwd_kernel(lgf_ref, q_ref, k_ref, v_ref, g_ref, km_ref, vm_ref, part_ref,
                    o_ref, s_ref):
    i = pl.program_id(1)
    pos = _chunk_pos()

    @pl.when(i == 0)
    def _():
        for h in range(RET_HEADS):
            k_dec = jnp.exp(lgf_ref[h] * (float(CHUNK - 1) - pos))
            kd = (km_ref[:, h * RET_DK:(h + 1) * RET_DK].astype(F32) * k_dec).astype(BF16)
            s_ref[h] = lax.dot_general(kd, vm_ref[:, h * RET_DV:(h + 1) * RET_DV], _TN,
                                       preferred_element_type=F32)

    n_chunks = q_ref.shape[0] // CHUNK
    heads = range(RET_HEADS)
    q_dec = [jnp.exp(lgf_ref[h] * (pos + 1.0)) for h in heads]
    k_dec = [jnp.exp(lgf_ref[h] * (float(CHUNK - 1) - pos)) for h in heads]
    c_dec = [jnp.exp(lgf_ref[h] * jnp.full((CHUNK, 1), float(CHUNK), F32)) for h in heads]
    state = [s_ref[h] for h in heads]
    for cc in range(n_chunks):
        rs = slice(cc * CHUNK, (cc + 1) * CHUNK)
        cols = [slice(h * RET_DV, (h + 1) * RET_DV) for h in heads]
        q = [q_ref[rs, h * RET_DK:(h + 1) * RET_DK] for h in heads]
        k = [k_ref[rs, h * RET_DK:(h + 1) * RET_DK] for h in heads]
        v = [v_ref[rs, cols[h]] for h in heads]
        cross = [jnp.dot(q[h], state[h].astype(BF16), preferred_element_type=F32) * q_dec[h]
                 for h in heads]
        kv = [lax.dot_general((k[h].astype(F32) * k_dec[h]).astype(BF16), v[h], _TN,
                              preferred_element_type=F32) for h in heads]
        for h in heads:
            tot = part_ref[rs, cols[h]] + cross[h]
            ms = jnp.mean(tot * tot, axis=-1, keepdims=True)
            gate = g_ref[rs, cols[h]].astype(F32)
            o_ref[rs, cols[h]] = (tot * lax.rsqrt(ms + EPS) * gate).astype(BF16)
        state = [state[h] * c_dec[h] + kv[h] for h in heads]
    for h in heads:
        s_ref[h] = state[h]


def _retention(proj, proj_meta, lgf, lgb, *, batch, seq):
    t = proj.shape[0]
    rows_blk = RET_STEP_CHUNKS * CHUNK
    n = seq // rows_blk
    smem = pl.BlockSpec(memory_space=pltpu.SMEM)
    params = pltpu.CompilerParams(dimension_semantics=("arbitrary", "arbitrary"),
                                  vmem_limit_bytes=VMEM_LIMIT)

    def rows_rev(b, i):
        return b * n + (n - 1 - i)

    part = pl.pallas_call(
        _ret_bwd_kernel,
        out_shape=jax.ShapeDtypeStruct((t, RET_V), F32),
        grid=(batch, n),
        in_specs=[
            smem, smem,
            pl.BlockSpec((rows_blk, RET_QK), lambda b, i: (rows_rev(b, i), OFF_RQ // RET_QK)),
            pl.BlockSpec((rows_blk, RET_QK), lambda b, i: (rows_rev(b, i), OFF_RK // RET_QK)),
            pl.BlockSpec((rows_blk, RET_V), lambda b, i: (rows_rev(b, i), OFF_RV // RET_V)),
        ],
        out_specs=pl.BlockSpec((rows_blk, RET_V), lambda b, i: (rows_rev(b, i), 0)),
        scratch_shapes=[pltpu.VMEM((RET_HEADS, RET_DK, RET_DV), F32),
                        pltpu.VMEM((RET_HEADS, CHUNK, CHUNK), F32)],
        compiler_params=params,
        name="retention_bwd",
    )(lgf, lgb, proj, proj, proj)

    def rows(b, i):
        return b * n + i

    return pl.pallas_call(
        _ret_fwd_kernel,
        out_shape=jax.ShapeDtypeStruct((t, RET_V), BF16),
        grid=(batch, n),
        in_specs=[
            smem,
            pl.BlockSpec((rows_blk, RET_QK), lambda b, i: (rows(b, i), OFF_RQ // RET_QK)),
            pl.BlockSpec((rows_blk, RET_QK), lambda b, i: (rows(b, i), OFF_RK // RET_QK)),
            pl.BlockSpec((rows_blk, RET_V), lambda b, i: (rows(b, i), OFF_RV // RET_V)),
            pl.BlockSpec((rows_blk, RET_V), lambda b, i: (rows(b, i), OFF_RG // RET_V)),
            pl.BlockSpec((CHUNK, RET_QK), lambda b, i: (0, OFF_RK // RET_QK)),
            pl.BlockSpec((CHUNK, RET_V), lambda b, i: (0, OFF_RV // RET_V)),
            pl.BlockSpec((rows_blk, RET_V), lambda b, i: (rows(b, i), 0)),
        ],
        out_specs=pl.BlockSpec((rows_blk, RET_V), lambda b, i: (rows(b, i), 0)),
        scratch_shapes=[pltpu.VMEM((RET_HEADS, RET_DK, RET_DV), F32)],
        compiler_params=params,
        name="retention_fwd",
    )(lgf, proj, proj, proj, proj, proj_meta, proj_meta, part)


def _attn_subtile(k, vt, mask, qt_ref, m_ref, l_ref, acc_ref, g, c):
    qs = slice(c * ATT_SUB, (c + 1) * ATT_SUB)
    st = jnp.dot(k, qt_ref[g * ATT_DH:(g + 1) * ATT_DH, qs], preferred_element_type=F32)
    if mask is not None:
        st = jnp.where(mask, st, -jnp.inf)
    m_prev = m_ref[g, :, qs]
    m_new = jnp.maximum(m_prev, jnp.max(st, axis=0, keepdims=True))
    alpha = jnp.exp2(m_prev - m_new)
    p = jnp.exp2(st - m_new)
    l_ref[g, :, qs] = alpha * l_ref[g, :, qs] + jnp.sum(p, axis=0, keepdims=True)
    acc_ref[g, :, qs] = alpha * acc_ref[g, :, qs] + jnp.dot(
        vt, p.astype(BF16), preferred_element_type=F32)
    m_ref[g, :, qs] = m_new


def _attn_kernel(qt_ref, k_ref, vt_ref, km_ref, vtm_ref, o_ref, m_ref, l_ref, acc_ref):
    ki = pl.program_id(3)
    tq = qt_ref.shape[1]
    tk = k_ref.shape[0]

    @pl.when(ki == 0)
    def _():
        m_ref[...] = jnp.full_like(m_ref, -jnp.inf)
        l_ref[...] = jnp.zeros_like(l_ref)
        acc_ref[...] = jnp.zeros_like(acc_ref)
        is_meta = lax.broadcasted_iota(jnp.int32, (CHUNK, ATT_SUB), 0) >= CHUNK - N_META
        for g in range(ATT_GROUP):
            for c in range(tq // ATT_SUB):
                _attn_subtile(km_ref[...], vtm_ref[...], is_meta, qt_ref, m_ref, l_ref, acc_ref,
                              g, c)

    for r in range(tk // ATT_SUB):
        ks = slice(r * ATT_SUB, (r + 1) * ATT_SUB)
        for g in range(ATT_GROUP):
            for c in range(tq // ATT_SUB):
                _attn_subtile(k_ref[ks, :], vt_ref[:, ks], None, qt_ref, m_ref, l_ref, acc_ref,
                              g, c)

    @pl.when(ki == pl.num_programs(3) - 1)
    def _():
        for g in range(ATT_GROUP):
            o = acc_ref[g] / l_ref[g]
            o_ref[:, g * ATT_DH:(g + 1) * ATT_DH] = o.T.astype(BF16)


def _attn_bounded_kernel(bound_ref, qt_ref, k_ref, vt_ref, km_ref, vtm_ref, o_ref, l_ref, acc_ref):
    tq = qt_ref.shape[1]
    bound = bound_ref[0]
    subs = [slice(c * ATT_SUB, (c + 1) * ATT_SUB) for c in range(tq // ATT_SUB)]

    def update(k, vt, heads, first):
        tiles = [(g, qs) for g in heads for qs in subs]
        sts = [jnp.dot(k, qt_ref[g * ATT_DH:(g + 1) * ATT_DH, qs], preferred_element_type=F32)
               for g, qs in tiles]
        ps = [jnp.exp2(st - bound) for st in sts]
        for (g, qs), p in zip(tiles, ps):
            l = p.reshape(-1, 8, ATT_SUB).sum(axis=0)
            l_ref[g, :, qs] = l if first else l_ref[g, :, qs] + l
        for (g, qs), p in zip(tiles, ps):
            a = jnp.dot(vt, p.astype(BF16), preferred_element_type=F32)
            acc_ref[g, :, qs] = a if first else acc_ref[g, :, qs] + a

    update(km_ref[...], vtm_ref[...], range(ATT_GROUP), True)

    def key_block(kb, carry):
        k = k_ref[pl.ds(pl.multiple_of(kb * ATT_KB, ATT_KB), ATT_KB), :]
        vt = vt_ref[kb]
        update(k, vt, range(ATT_GROUP), False)
        return carry

    lax.fori_loop(0, vt_ref.shape[0], key_block, 0)

    for g in range(ATT_GROUP):
        o = acc_ref[g] / jnp.sum(l_ref[g], axis=0, keepdims=True)
        o_ref[:, g * ATT_DH:(g + 1) * ATT_DH] = o.T.astype(BF16)


def _attention_bounded(qt, ak, vt, ak_meta, vt_meta, bound, *, batch, seq, tq):
    t = ak.shape[0]
    nq = seq // tq
    nkb = seq // ATT_KB
    qw = ATT_GROUP * ATT_DH
    km = ak_meta[CHUNK - N_META:]
    vtm = vt_meta[0][:, CHUNK - N_META:]
    return pl.pallas_call(
        _attn_bounded_kernel,
        out_shape=jax.ShapeDtypeStruct((t, ATT_Q), BF16),
        grid=(batch, ATT_KV_HEADS, nq),
        in_specs=[
            pl.BlockSpec(memory_space=pltpu.SMEM),
            pl.BlockSpec((qw, tq), lambda b, h, qi: (h, b * nq + qi)),
            pl.BlockSpec((seq, ATT_DH), lambda b, h, qi: (b, h)),
            pl.BlockSpec((nkb, ATT_DH, ATT_KB), lambda b, h, qi: (b, h, 0)),
            pl.BlockSpec((N_META, ATT_DH), lambda b, h, qi: (0, h)),
            pl.BlockSpec((ATT_DH, N_META), lambda b, h, qi: (h, 0)),
        ],
        out_specs=pl.BlockSpec((tq, qw), lambda b, h, qi: (b * nq + qi, h)),
        scratch_shapes=[pltpu.VMEM((ATT_GROUP, 8, tq), F32),
                        pltpu.VMEM((ATT_GROUP, ATT_DH, tq), F32)],
        compiler_params=pltpu.CompilerParams(
            dimension_semantics=("arbitrary",) * 3, vmem_limit_bytes=VMEM_LIMIT),
        name="attention_bounded",
    )(bound, qt, ak, vt, km, vtm)


def _attention_online(qt, ak, vt, ak_meta, vt_meta, *, batch, seq, tq):
    t = ak.shape[0]
    nq = seq // tq
    nk = seq // ATT_KB
    qw = ATT_GROUP * ATT_DH
    return pl.pallas_call(
        _attn_kernel,
        out_shape=jax.ShapeDtypeStruct((t, ATT_Q), BF16),
        grid=(batch, ATT_KV_HEADS, nq, nk),
        in_specs=[
            pl.BlockSpec((qw, tq), lambda b, h, qi, ki: (h, b * nq + qi)),
            pl.BlockSpec((ATT_KB, ATT_DH), lambda b, h, qi, ki: (b * nk + ki, h)),
            pl.BlockSpec((None, ATT_DH, ATT_KB), lambda b, h, qi, ki: (b * nk + ki, h, 0)),
            pl.BlockSpec((CHUNK, ATT_DH), lambda b, h, qi, ki: (0, h)),
            pl.BlockSpec((None, ATT_DH, CHUNK), lambda b, h, qi, ki: (0, h, 0)),
        ],
        out_specs=pl.BlockSpec((tq, qw), lambda b, h, qi, ki: (b * nq + qi, h)),
        scratch_shapes=[pltpu.VMEM((ATT_GROUP, 1, tq), F32),
                        pltpu.VMEM((ATT_GROUP, 1, tq), F32),
                        pltpu.VMEM((ATT_GROUP, ATT_DH, tq), F32)],
        compiler_params=pltpu.CompilerParams(
            dimension_semantics=("arbitrary",) * 4, vmem_limit_bytes=VMEM_LIMIT),
        name="attention_online",
    )(qt, ak, vt, ak_meta, vt_meta)


def _out_proj_kernel(x_ref, r_ref, a_ref, wr_ref, wa_ref, ln_ref, h_ref, u_ref):
    h = (x_ref[...]
         + jnp.dot(r_ref[...], wr_ref[...], preferred_element_type=F32)
         + jnp.dot(a_ref[...], wa_ref[...], preferred_element_type=F32))
    h_ref[...] = h
    ms = jnp.mean(h * h, axis=-1, keepdims=True)
    u_ref[...] = (h * lax.rsqrt(ms + EPS) * ln_ref[...]).astype(BF16)


def _out_proj(x, ret, att, w_out, ln2, *, tm):
    t = x.shape[0]
    row = lambda w: pl.BlockSpec((tm, w), lambda i: (i, 0))
    return pl.pallas_call(
        _out_proj_kernel,
        out_shape=(jax.ShapeDtypeStruct((t, D_MODEL), F32),
                   jax.ShapeDtypeStruct((t, D_MODEL), BF16)),
        grid=(t // tm,),
        in_specs=[
            row(D_MODEL), row(RET_V), row(ATT_Q),
            pl.BlockSpec((RET_V, D_MODEL), lambda i: (0, 0)),
            pl.BlockSpec((ATT_Q, D_MODEL), lambda i: (1, 0)),
            pl.BlockSpec((1, D_MODEL), lambda i: (0, 0)),
        ],
        out_specs=(row(D_MODEL), row(D_MODEL)),
        compiler_params=pltpu.CompilerParams(
            dimension_semantics=("arbitrary",), vmem_limit_bytes=VMEM_LIMIT),
        name="out_proj",
    )(x, ret, att, w_out, w_out, ln2)


def _mlp_kernel(u_ref, h_ref, wu_ref, wd_ref, fg_ref, o_ref):
    j = pl.program_id(1)
    last = pl.num_programs(1) - 1
    tm = u_ref.shape[0]

    def block(rows, first, final):
        a = jnp.dot(u_ref[rows, :], wu_ref[...], preferred_element_type=F32)
        a = jnp.square(jnp.maximum(a, 0.0)).astype(BF16)
        base = h_ref[rows, :] if first else o_ref[rows, :]
        h = base + jnp.dot(a, wd_ref[...], preferred_element_type=F32)
        if final:
            ms = jnp.mean(h * h, axis=-1, keepdims=True)
            h = h * lax.rsqrt(ms + EPS) * fg_ref[...]
        o_ref[rows, :] = h

    everything = slice(0, tm)
    pl.when(j == 0)(lambda: block(everything, True, False))
    pl.when((j > 0) & (j < last))(lambda: block(everything, False, False))

    @pl.when(j == last)
    def _():
        for r in range(2):
            block(slice(r * tm // 2, (r + 1) * tm // 2), False, True)


def _mlp(u, h, w_up, w_down, fg, *, tm, tf):
    t = u.shape[0]
    assert D_FF // tf >= 2
    return pl.pallas_call(
        _mlp_kernel,
        out_shape=jax.ShapeDtypeStruct((t, D_MODEL), F32),
        grid=(t // tm, D_FF // tf),
        in_specs=[
            pl.BlockSpec((tm, D_MODEL), lambda i, j: (i, 0)),
            pl.BlockSpec((tm, D_MODEL), lambda i, j: (i, 0)),
            pl.BlockSpec((D_MODEL, tf), lambda i, j: (0, j)),
            pl.BlockSpec((tf, D_MODEL), lambda i, j: (j, 0)),
            pl.BlockSpec((1, D_MODEL), lambda i, j: (0, 0)),
        ],
        out_specs=pl.BlockSpec((tm, D_MODEL), lambda i, j: (i, 0)),
        compiler_params=pltpu.CompilerParams(
            dimension_semantics=("arbitrary", "arbitrary"), vmem_limit_bytes=VMEM_LIMIT),
        name="mlp",
    )(u, h, w_up, w_down, fg)


def _rope_tables(seq):
    pairs = lambda t: jnp.repeat(t, 2, axis=-1)
    sign = jnp.tile(jnp.array([-1.0, 1.0], F32), 64)
    n_chunks = seq // CHUNK + 1
    freq_r = ROPE_THETA ** (-jnp.linspace(0.0, 1.0, RET_DK // 2, dtype=F32))
    ang_hi = (jnp.arange(n_chunks, dtype=F32) * float(CHUNK))[:, None] * freq_r[None]
    ang_lo = jnp.arange(CHUNK, dtype=F32)[:, None] * freq_r[None]
    rhi = jnp.stack([pairs(jnp.cos(ang_hi)), pairs(jnp.sin(ang_hi))])
    c_lo, s_lo = pairs(jnp.cos(ang_lo)), pairs(jnp.sin(ang_lo))
    rlo = jnp.stack([c_lo, s_lo, sign * c_lo, sign * s_lo])
    n_pair = ATT_DH // 4
    freq_a = ROPE_THETA ** (-jnp.arange(n_pair, dtype=F32) / n_pair)
    ang_row = jnp.arange(seq // GRID_W, dtype=F32)[:, None] * freq_a[None]
    ang_col = jnp.arange(GRID_W, dtype=F32)[:, None] * freq_a[None]
    in_row = lambda t: jnp.concatenate([pairs(t), jnp.zeros_like(pairs(t))], axis=-1)
    in_col = lambda t: jnp.concatenate([jnp.zeros_like(pairs(t)), pairs(t)], axis=-1)
    arow = jnp.stack([in_row(jnp.cos(ang_row)), sign * in_row(jnp.sin(ang_row))])
    acol = jnp.stack([in_col(jnp.cos(ang_col)), sign * in_col(jnp.sin(ang_col))])
    return rhi, rlo, arow, acol


def _project(x, wts, tabs, meta):
    batch, seq, _ = x.shape
    for tile in (IN_PROJ_TM, RET_STEP_CHUNKS * CHUNK, ATT_TQ, ATT_KB, OUT_PROJ_TM, MLP_TM):
        assert seq % tile == 0, (seq, tile)
    x2 = x.reshape(batch * seq, D_MODEL)
    proj, qt, ak, vt = _in_proj(x2, wts["ln1"], wts["w_in"], *tabs, wts["qg"], wts["kg"],
                                tm=IN_PROJ_TM, table_blocks=seq // IN_PROJ_TM)
    ret = _retention(proj, meta[0], wts["lgf"], wts["lgb"], batch=batch, seq=seq)
    return x2, ret, (qt, ak, vt)


def _attend(qkv, wts, meta, shape, bounded):
    batch, seq, _ = shape
    _, ak_meta, vt_meta = meta
    if bounded:
        return _attention_bounded(*qkv, ak_meta, vt_meta, wts["att_bound"],
                                  batch=batch, seq=seq, tq=ATT_TQ)
    return _attention_online(*qkv, ak_meta, vt_meta, batch=batch, seq=seq, tq=ATT_TQ)


def _finish(x2, ret, att, wts, shape):
    h1, u2 = _out_proj(x2, ret, att, wts["w_out"], wts["ln2"], tm=OUT_PROJ_TM)
    y = _mlp(u2, h1, wts["w_up"], wts["w_down"], wts["fg"], tm=MLP_TM, tf=MLP_TF)
    return y.reshape(shape)


def kernel(x_prompt, x_sample, meta_tokens, ln1_g, w_in, q_norm_g, k_norm_g, ret_log_decay_fwd,
           ret_log_decay_bwd, w_out, ln2_g, w_up, w_down, final_norm_g):
    assert w_in.shape[0] == 1, "meta-token residual stream is only skippable for a single layer"
    wts = {
        "ln1": ln1_g[0].reshape(1, D_MODEL),
        "w_in": w_in[0].astype(BF16),
        "qg": q_norm_g[0].reshape(1, ATT_DH),
        "kg": k_norm_g[0].reshape(1, ATT_DH),
        "lgf": ret_log_decay_fwd[0],
        "lgb": ret_log_decay_bwd[0],
        "w_out": w_out[0].astype(BF16),
        "ln2": ln2_g[0].reshape(1, D_MODEL),
        "w_up": w_up[0].astype(BF16),
        "w_down": w_down[0].astype(BF16),
        "fg": final_norm_g.reshape(1, D_MODEL),
    }
    wts["att_bound"] = (1.02 * LOG2E * ATT_DH ** 0.5 * jnp.max(jnp.abs(q_norm_g[0]))
                        * jnp.max(jnp.abs(k_norm_g[0]))).reshape(1).astype(F32)
    max_seq = max(x_prompt.shape[1], x_sample.shape[1])
    rhi, rlo, arow, acol = _rope_tables(max_seq)

    x_meta = jnp.concatenate(
        [jnp.zeros((CHUNK - N_META, D_MODEL), F32), meta_tokens.astype(F32)], axis=0)
    unrotated = jnp.stack([jnp.ones((CHUNK // GRID_W, LANES), F32),
                           jnp.zeros((CHUNK // GRID_W, LANES), F32)])
    proj_meta, _, ak_meta, vt_meta = _in_proj(
        x_meta, wts["ln1"], wts["w_in"], rhi[:, :1], rlo, unrotated, jnp.zeros_like(acol),
        wts["qg"], wts["kg"], tm=CHUNK, table_blocks=1)
    meta = (proj_meta, ak_meta, vt_meta)

    tabs = (rhi[:, 1:], rlo, arow, acol)
    xs = (x_prompt, x_sample)
    x2s, rets, qkvs = zip(*[_project(x, wts, tabs, meta) for x in xs])
    atts = lax.cond(
        wts["att_bound"][0] <= ATT_BOUND_MAX,
        lambda: tuple(_attend(qkv, wts, meta, x.shape, True) for qkv, x in zip(qkvs, xs)),
        lambda: tuple(_attend(qkv, wts, meta, x.shape, False) for qkv, x in zip(qkvs, xs)))
    return tuple(_finish(x2, ret, att, wts, x.shape)
                 for x2, ret, att, x in zip(x2s, rets, atts, xs))
```

```python
import jax
import jax.numpy as jnp
from jax import lax
from jax.experimental import pallas as pl
from jax.experimental.pallas import tpu as pltpu

F32 = jnp.float32
BF16 = jnp.bfloat16

D_MODEL = 2048
N_META = 16
GRID_W = 64
CHUNK = 128
RET_HEADS = 4
RET_DK = 128
RET_DV = 256
RET_QK = RET_HEADS * RET_DK
RET_V = RET_HEADS * RET_DV
ATT_DH = 128
ATT_HEADS = 8
ATT_KV_HEADS = 2
ATT_GROUP = ATT_HEADS // ATT_KV_HEADS
ATT_Q = ATT_HEADS * ATT_DH
ATT_KV = ATT_KV_HEADS * ATT_DH
D_FF = 4 * D_MODEL
IN_W = 2 * RET_QK + 2 * RET_V + ATT_Q + 2 * ATT_KV
ROPE_THETA = 10000.0
EPS = 1e-6
LOG2E = 1.4426950408889634

OFF_RQ = 0
OFF_RK = OFF_RQ + RET_QK
OFF_RV = OFF_RK + RET_QK
OFF_RG = OFF_RV + RET_V
OFF_AQ = OFF_RG + RET_V
OFF_AK = OFF_AQ + ATT_Q
OFF_AV = OFF_AK + ATT_KV

LANES = 128
V7X_VMEM_BYTES = 64 * 1024 * 1024
VMEM_LIMIT = V7X_VMEM_BYTES * 7 // 8

PROJ_TN = 512
N_RET_BLOCKS = OFF_AQ // PROJ_TN
IN_PROJ_TM = 2048
IN_PROJ_SUB_ROWS = 256
RET_STEP_CHUNKS = 8
ATT_TQ = 1024
ATT_SUB = 256
ATT_KB = 2048
OUT_PROJ_TM = 512
MLP_TM = 512
MLP_TF = 2048
ATT_BOUND_MAX = 60.0

_NT = (((1,), (1,)), ((), ()))
_TN = (((0,), (0,)), ((), ()))


def _rope_pairs(a, c, s):
    even = lax.broadcasted_iota(jnp.int32, (a.shape[0], LANES), 1) % 2 == 0
    outs = []
    for h in range(a.shape[1] // LANES):
        ah = a[:, h * LANES:(h + 1) * LANES]
        partner = jnp.where(even, pltpu.roll(ah, LANES - 1, 1), pltpu.roll(ah, 1, 1))
        outs.append(ah * c + partner * s)
    return outs[0] if len(outs) == 1 else jnp.concatenate(outs, axis=1)


def _head_norm(a, g):
    outs = []
    for h in range(a.shape[1] // LANES):
        ah = a[:, h * LANES:(h + 1) * LANES]
        ms = jnp.mean(ah * ah, axis=-1, keepdims=True)
        outs.append(ah * lax.rsqrt(ms + EPS) * g)
    return outs[0] if len(outs) == 1 else jnp.concatenate(outs, axis=1)


def _in_proj_kernel(x_hbm, ln_ref, w_ref, rhi_ref, rlo_ref, arow_ref, acol_ref, qg_ref, kg_ref,
                    o_ref, qt_ref, ak_ref, vt_ref, u_ref, x_ref, x_sem):
    j = pl.program_id(1)
    tm = u_ref.shape[0]
    sub = min(tm, IN_PROJ_SUB_ROWS)

    def ret_tables(rs):
        cs, ss = [], []
        for a in range(rs.start // CHUNK, rs.stop // CHUNK):
            hc, hs = rhi_ref[0, a:a + 1, :], rhi_ref[1, a:a + 1, :]
            cs.append(hc * rlo_ref[0] - hs * rlo_ref[1])
            ss.append(hs * rlo_ref[2] + hc * rlo_ref[3])
        return jnp.concatenate(cs, axis=0), jnp.concatenate(ss, axis=0)

    def axial_tables(rs):
        rows = range(rs.start // GRID_W, rs.stop // GRID_W)
        return (jnp.concatenate([arow_ref[0, g:g + 1, :] + acol_ref[0] for g in rows], axis=0),
                jnp.concatenate([arow_ref[1, g:g + 1, :] + acol_ref[1] for g in rows], axis=0))

    def run(epilogue, normalize=False):
        for r in range(tm // sub):
            rs = slice(r * sub, (r + 1) * sub)
            if normalize:
                x = x_ref[rs, :]
                ms = jnp.mean(x * x, axis=-1, keepdims=True)
                u_ref[rs, :] = (x * lax.rsqrt(ms + EPS) * ln_ref[...]).astype(BF16)
            acc = jnp.dot(u_ref[rs, :], w_ref[...], preferred_element_type=F32)
            epilogue(acc, rs)

    def ret_q(acc, rs):
        o_ref[rs, :] = _rope_pairs(acc, *ret_tables(rs)).astype(BF16)

    def ret_k(acc, rs):
        o_ref[rs, :] = (_rope_pairs(acc, *ret_tables(rs)) * (RET_DK ** -0.5)).astype(BF16)

    def plain(acc, rs):
        o_ref[rs, :] = acc.astype(BF16)

    def att_q(acc, rs):
        qn = _head_norm(acc, qg_ref[...])
        q = _rope_pairs(qn, *axial_tables(rs)) * (ATT_DH ** -0.5 * LOG2E)
        qt_ref[:, rs] = q.T.astype(BF16)

    def att_kv(acc, rs):
        kn = _head_norm(acc[:, :ATT_KV], kg_ref[...])
        ak_ref[rs, :] = _rope_pairs(kn, *axial_tables(rs)).astype(BF16)
        kb = vt_ref.shape[2]
        vt_ref[rs.start // kb, :, rs.start % kb:rs.start % kb + sub] = (
            acc[:, ATT_KV:].T.astype(BF16))

    i = pl.program_id(0)

    def x_copy(tile):
        return pltpu.make_async_copy(x_hbm.at[pl.ds(tile * tm, tm), :], x_ref, x_sem)

    @pl.when(j == OFF_RQ // PROJ_TN)
    def _():
        pl.when(i == 0)(lambda: x_copy(0).start())
        x_copy(i).wait()
        run(ret_q, normalize=True)

    @pl.when(j == OFF_RK // PROJ_TN)
    def _():
        pl.when(i + 1 < pl.num_programs(0))(lambda: x_copy(i + 1).start())
        run(ret_k)

    pl.when((j >= OFF_RV // PROJ_TN) & (j < N_RET_BLOCKS))(lambda: run(plain))
    pl.when((j >= OFF_AQ // PROJ_TN) & (j < OFF_AK // PROJ_TN))(lambda: run(att_q))
    pl.when(j == OFF_AK // PROJ_TN)(lambda: run(att_kv))


def _in_proj(x, ln1, w_in, rhi, rlo, arow, acol, qg, kg, *, tm, table_blocks):
    t = x.shape[0]
    kb = min(tm, ATT_KB)
    n_j = IN_W // PROJ_TN
    j_aq = OFF_AQ // PROJ_TN
    seq_tab = lambda a, n: pl.BlockSpec((a.shape[0], n, LANES), lambda i, j: (0, i % table_blocks, 0))
    whole = lambda a: pl.BlockSpec(a.shape, lambda i, j: (0, 0, 0))
    vec = lambda n: pl.BlockSpec((1, n), lambda i, j: (0, 0))
    return pl.pallas_call(
        _in_proj_kernel,
        out_shape=(jax.ShapeDtypeStruct((t, OFF_AQ), BF16),
                   jax.ShapeDtypeStruct((ATT_Q, t), BF16),
                   jax.ShapeDtypeStruct((t, ATT_KV), BF16),
                   jax.ShapeDtypeStruct((t // kb, ATT_KV, kb), BF16)),
        grid=(t // tm, n_j),
        in_specs=[
            pl.BlockSpec(memory_space=pl.ANY),
            vec(D_MODEL),
            pl.BlockSpec((D_MODEL, PROJ_TN), lambda i, j: (0, j)),
            seq_tab(rhi, tm // CHUNK), whole(rlo), seq_tab(arow, tm // GRID_W), whole(acol),
            vec(ATT_DH), vec(ATT_DH),
        ],
        out_specs=(
            pl.BlockSpec((tm, PROJ_TN), lambda i, j: (i, jnp.minimum(j, N_RET_BLOCKS - 1))),
            pl.BlockSpec((PROJ_TN, tm), lambda i, j: (jnp.clip(j - j_aq, 0, 1), i)),
            pl.BlockSpec((tm, ATT_KV), lambda i, j: (i, 0)),
            pl.BlockSpec((tm // kb, ATT_KV, kb), lambda i, j: (i, 0, 0)),
        ),
        scratch_shapes=[pltpu.VMEM((tm, D_MODEL), BF16), pltpu.VMEM((tm, D_MODEL), F32),
                        pltpu.SemaphoreType.DMA],
        compiler_params=pltpu.CompilerParams(
            dimension_semantics=("arbitrary", "arbitrary"), vmem_limit_bytes=VMEM_LIMIT),
        name="in_proj",
    )(x, ln1, w_in, rhi, rlo, arow, acol, qg, kg)


def _chunk_pos():
    return lax.broadcasted_iota(jnp.int32, (CHUNK, 1), 0).astype(F32)


def _ret_bwd_kernel(lgf_ref, lgb_ref, q_ref, k_ref, v_ref, part_ref, s_ref, d_ref):
    b = pl.program_id(0)
    i = pl.program_id(1)

    @pl.when((b == 0) & (i == 0))
    def _():
        r = lax.broadcasted_iota(jnp.int32, (CHUNK, CHUNK), 0)
        c = lax.broadcasted_iota(jnp.int32, (CHUNK, CHUNK), 1)
        diff = (r - c).astype(F32)
        for h in range(RET_HEADS):
            df = jnp.where(diff >= 0, jnp.exp(lgf_ref[h] * jnp.maximum(diff, 0.0)), 0.0)
            db = jnp.where(diff < 0, jnp.exp(lgb_ref[h] * jnp.maximum(-diff, 0.0)), 0.0)
            d_ref[h] = df + db

    @pl.when(i == 0)
    def _():
        s_ref[...] = jnp.zeros_like(s_ref)

    pos = _chunk_pos()
    n_chunks = q_ref.shape[0] // CHUNK
    heads = range(RET_HEADS)
    q_dec = [jnp.exp(lgb_ref[h] * (float(CHUNK) - pos)) for h in heads]
    k_dec = [jnp.exp(lgb_ref[h] * pos) for h in heads]
    c_dec = [jnp.exp(lgb_ref[h] * jnp.full((CHUNK, 1), float(CHUNK), F32)) for h in heads]
    state = [s_ref[h] for h in heads]
    for cc in reversed(range(n_chunks)):
        rs = slice(cc * CHUNK, (cc + 1) * CHUNK)
        q = [q_ref[rs, h * RET_DK:(h + 1) * RET_DK] for h in heads]
        k = [k_ref[rs, h * RET_DK:(h + 1) * RET_DK] for h in heads]
        v = [v_ref[rs, h * RET_DV:(h + 1) * RET_DV] for h in heads]
        sc = [lax.dot_general(q[h], k[h], _NT, preferred_element_type=F32) * d_ref[h]
              for h in heads]
        lhs = [jnp.concatenate([sc[h].astype(BF16),
                                (q[h].astype(F32) * q_dec[h]).astype(BF16)], axis=1)
               for h in heads]
        rhs = [jnp.concatenate([v[h], state[h].astype(BF16)], axis=0) for h in heads]
        for h in heads:
            part_ref[rs, h * RET_DV:(h + 1) * RET_DV] = jnp.dot(
                lhs[h], rhs[h], preferred_element_type=F32)
        kv = [lax.dot_general((k[h].astype(F32) * k_dec[h]).astype(BF16), v[h], _TN,
                              preferred_element_type=F32) for h in heads]
        state = [state[h] * c_dec[h] + kv[h] for h in heads]
    for h in heads:
        s_ref[h] = state[h]


def _ret_fwd_kernel(lgf_ref, q_ref, k_ref, v_ref, g_ref, km_ref, vm_ref, part_ref,
                    o_ref, s_ref):
    i = pl.program_id(1)
    pos = _chunk_pos()

    @pl.when(i == 0)
    def _():
        for h in range(RET_HEADS):
            k_dec = jnp.exp(lgf_ref[h] * (float(CHUNK - 1) - pos))
            kd = (km_ref[:, h * RET_DK:(h + 1) * RET_DK].astype(F32) * k_dec).astype(BF16)
            s_ref[h] = lax.dot_general(kd, vm_ref[:, h * RET_DV:(h + 1) * RET_DV], _TN,
                                       preferred_element_type=F32)

    n_chunks = q_ref.shape[0] // CHUNK
    heads = range(RET_HEADS)
    q_dec = [jnp.exp(lgf_ref[h] * (pos + 1.0)) for h in heads]
    k_dec = [jnp.exp(lgf_ref[h] * (float(CHUNK - 1) - pos)) for h in heads]
    c_dec = [jnp.exp(lgf_ref[h] * jnp.full((CHUNK, 1), float(CHUNK), F32)) for h in heads]
    state = [s_ref[h] for h in heads]
    for cc in range(n_chunks):
        rs = slice(cc * CHUNK, (cc + 1) * CHUNK)
        cols = [slice(h * RET_DV, (h + 1) * RET_DV) for h in heads]
        q = [q_ref[rs, h * RET_DK:(h + 1) * RET_DK] for h in heads]
        k = [k_ref[rs, h * RET_DK:(h + 1) * RET_DK] for h in heads]
        v = [v_ref[rs, cols[h]] for h in heads]
        cross = [jnp.dot(q[h], state[h].astype(BF16), preferred_element_type=F32) * q_dec[h]
                 for h in heads]
        kv = [lax.dot_general((k[h].astype(F32) * k_dec[h]).astype(BF16), v[h], _TN,
                              preferred_element_type=F32) for h in heads]
        for h in heads:
            tot = part_ref[rs, cols[h]] + cross[h]
            ms = jnp.mean(tot * tot, axis=-1, keepdims=True)
            g = g_ref[rs, cols[h]].astype(F32)
            gate = g / (1.0 + jnp.exp(-g))
            o_ref[rs, cols[h]] = (tot * lax.rsqrt(ms + EPS) * gate).astype(BF16)
        state = [state[h] * c_dec[h] + kv[h] for h in heads]
    for h in heads:
        s_ref[h] = state[h]


def _retention(proj, proj_meta, lgf, lgb, *, batch, seq):
    t = proj.shape[0]
    rows_blk = RET_STEP_CHUNKS * CHUNK
    n = seq // rows_blk
    smem = pl.BlockSpec(memory_space=pltpu.SMEM)
    params = pltpu.CompilerParams(dimension_semantics=("arbitrary", "arbitrary"),
                                  vmem_limit_bytes=VMEM_LIMIT)

    def rows_rev(b, i):
        return b * n + (n - 1 - i)

    part = pl.pallas_call(
        _ret_bwd_kernel,
        out_shape=jax.ShapeDtypeStruct((t, RET_V), F32),
        grid=(batch, n),
        in_specs=[
            smem, smem,
            pl.BlockSpec((rows_blk, RET_QK), lambda b, i: (rows_rev(b, i), OFF_RQ // RET_QK)),
            pl.BlockSpec((rows_blk, RET_QK), lambda b, i: (rows_rev(b, i), OFF_RK // RET_QK)),
            pl.BlockSpec((rows_blk, RET_V), lambda b, i: (rows_rev(b, i), OFF_RV // RET_V)),
        ],
        out_specs=pl.BlockSpec((rows_blk, RET_V), lambda b, i: (rows_rev(b, i), 0)),
        scratch_shapes=[pltpu.VMEM((RET_HEADS, RET_DK, RET_DV), F32),
                        pltpu.VMEM((RET_HEADS, CHUNK, CHUNK), F32)],
        compiler_params=params,
        name="retention_bwd",
    )(lgf, lgb, proj, proj, proj)

    def rows(b, i):
        return b * n + i

    return pl.pallas_call(
        _ret_fwd_kernel,
        out_shape=jax.ShapeDtypeStruct((t, RET_V), BF16),
        grid=(batch, n),
        in_specs=[
            smem,
            pl.BlockSpec((rows_blk, RET_QK), lambda b, i: (rows(b, i), OFF_RQ // RET_QK)),
            pl.BlockSpec((rows_blk, RET_QK), lambda b, i: (rows(b, i), OFF_RK // RET_QK)),
            pl.BlockSpec((rows_blk, RET_V), lambda b, i: (rows(b, i), OFF_RV // RET_V)),
            pl.BlockSpec((rows_blk, RET_V), lambda b, i: (rows(b, i), OFF_RG // RET_V)),
            pl.BlockSpec((CHUNK, RET_QK), lambda b, i: (0, OFF_RK // RET_QK)),
            pl.BlockSpec((CHUNK, RET_V), lambda b, i: (0, OFF_RV // RET_V)),
            pl.BlockSpec((rows_blk, RET_V), lambda b, i: (rows(b, i), 0)),
        ],
        out_specs=pl.BlockSpec((rows_blk, RET_V), lambda b, i: (rows(b, i), 0)),
        scratch_shapes=[pltpu.VMEM((RET_HEADS, RET_DK, RET_DV), F32)],
        compiler_params=params,
        name="retention_fwd",
    )(lgf, proj, proj, proj, proj, proj_meta, proj_meta, part)


def _attn_subtile(k, vt, mask, qt_ref, m_ref, l_ref, acc_ref, g, c):
    qs = slice(c * ATT_SUB, (c + 1) * ATT_SUB)
    st = jnp.dot(k, qt_ref[g * ATT_DH:(g + 1) * ATT_DH, qs], preferred_element_type=F32)
    if mask is not None:
        st = jnp.where(mask, st, -jnp.inf)
    m_prev = m_ref[g, :, qs]
    m_new = jnp.maximum(m_prev, jnp.max(st, axis=0, keepdims=True))
    alpha = jnp.exp2(m_prev - m_new)
    p = jnp.exp2(st - m_new)
    l_ref[g, :, qs] = alpha * l_ref[g, :, qs] + jnp.sum(p, axis=0, keepdims=True)
    acc_ref[g, :, qs] = alpha * acc_ref[g, :, qs] + jnp.dot(
        vt, p.astype(BF16), preferred_element_type=F32)
    m_ref[g, :, qs] = m_new


def _attn_kernel(qt_ref, k_ref, vt_ref, km_ref, vtm_ref, o_ref, m_ref, l_ref, acc_ref):
    ki = pl.program_id(3)
    tq = qt_ref.shape[1]
    tk = k_ref.shape[0]

    @pl.when(ki == 0)
    def _():
        m_ref[...] = jnp.full_like(m_ref, -jnp.inf)
        l_ref[...] = jnp.zeros_like(l_ref)
        acc_ref[...] = jnp.zeros_like(acc_ref)
        is_meta = lax.broadcasted_iota(jnp.int32, (CHUNK, ATT_SUB), 0) >= CHUNK - N_META
        for g in range(ATT_GROUP):
            for c in range(tq // ATT_SUB):
                _attn_subtile(km_ref[...], vtm_ref[...], is_meta, qt_ref, m_ref, l_ref, acc_ref,
                              g, c)

    for r in range(tk // ATT_SUB):
        ks = slice(r * ATT_SUB, (r + 1) * ATT_SUB)
        for g in range(ATT_GROUP):
            for c in range(tq // ATT_SUB):
                _attn_subtile(k_ref[ks, :], vt_ref[:, ks], None, qt_ref, m_ref, l_ref, acc_ref,
                              g, c)

    @pl.when(ki == pl.num_programs(3) - 1)
    def _():
        for g in range(ATT_GROUP):
            o = acc_ref[g] / l_ref[g]
            o_ref[:, g * ATT_DH:(g + 1) * ATT_DH] = o.T.astype(BF16)


def _attn_bounded_kernel(bound_ref, qt_ref, k_ref, vt_ref, km_ref, vtm_ref, o_ref, l_ref, acc_ref):
    tq = qt_ref.shape[1]
    bound = bound_ref[0]
    subs = [slice(c * ATT_SUB, (c + 1) * ATT_SUB) for c in range(tq // ATT_SUB)]

    def update(k, vt, heads, first):
        tiles = [(g, qs) for g in heads for qs in subs]
        sts = [jnp.dot(k, qt_ref[g * ATT_DH:(g + 1) * ATT_DH, qs], preferred_element_type=F32)
               for g, qs in tiles]
        ps = [jnp.exp2(st - bound) for st in sts]
        for (g, qs), p in zip(tiles, ps):
            l = p.reshape(-1, 8, ATT_SUB).sum(axis=0)
            l_ref[g, :, qs] = l if first else l_ref[g, :, qs] + l
        for (g, qs), p in zip(tiles, ps):
            a = jnp.dot(vt, p.astype(BF16), preferred_element_type=F32)
            acc_ref[g, :, qs] = a if first else acc_ref[g, :, qs] + a

    update(km_ref[...], vtm_ref[...], range(ATT_GROUP), True)

    def key_block(kb, carry):
        k = k_ref[pl.ds(pl.multiple_of(kb * ATT_KB, ATT_KB), ATT_KB), :]
        vt = vt_ref[kb]
        update(k, vt, range(ATT_GROUP), False)
        return carry

    lax.fori_loop(0, vt_ref.shape[0], key_block, 0)

    for g in range(ATT_GROUP):
        o = acc_ref[g] / jnp.sum(l_ref[g], axis=0, keepdims=True)
        o_ref[:, g * ATT_DH:(g + 1) * ATT_DH] = o.T.astype(BF16)


def _attention_bounded(qt, ak, vt, ak_meta, vt_meta, bound, *, batch, seq, tq):
    t = ak.shape[0]
    nq = seq // tq
    nkb = seq // ATT_KB
    qw = ATT_GROUP * ATT_DH
    km = ak_meta[CHUNK - N_META:]
    vtm = vt_meta[0][:, CHUNK - N_META:]
    return pl.pallas_call(
        _attn_bounded_kernel,
        out_shape=jax.ShapeDtypeStruct((t, ATT_Q), BF16),
        grid=(batch, ATT_KV_HEADS, nq),
        in_specs=[
            pl.BlockSpec(memory_space=pltpu.SMEM),
            pl.BlockSpec((qw, tq), lambda b, h, qi: (h, b * nq + qi)),
            pl.BlockSpec((seq, ATT_DH), lambda b, h, qi: (b, h)),
            pl.BlockSpec((nkb, ATT_DH, ATT_KB), lambda b, h, qi: (b, h, 0)),
            pl.BlockSpec((N_META, ATT_DH), lambda b, h, qi: (0, h)),
            pl.BlockSpec((ATT_DH, N_META), lambda b, h, qi: (h, 0)),
        ],
        out_specs=pl.BlockSpec((tq, qw), lambda b, h, qi: (b * nq + qi, h)),
        scratch_shapes=[pltpu.VMEM((ATT_GROUP, 8, tq), F32),
                        pltpu.VMEM((ATT_GROUP, ATT_DH, tq), F32)],
        compiler_params=pltpu.CompilerParams(
            dimension_semantics=("arbitrary",) * 3, vmem_limit_bytes=VMEM_LIMIT),
        name="attention_bounded",
    )(bound, qt, ak, vt, km, vtm)


def _attention_online(qt, ak, vt, ak_meta, vt_meta, *, batch, seq, tq):
    t = ak.shape[0]
    nq = seq // tq
    nk = seq // ATT_KB
    qw = ATT_GROUP * ATT_DH
    return pl.pallas_call(
        _attn_kernel,
        out_shape=jax.ShapeDtypeStruct((t, ATT_Q), BF16),
        grid=(batch, ATT_KV_HEADS, nq, nk),
        in_specs=[
            pl.BlockSpec((qw, tq), lambda b, h, qi, ki: (h, b * nq + qi)),
            pl.BlockSpec((ATT_KB, ATT_DH), lambda b, h, qi, ki: (b * nk + ki, h)),
            pl.BlockSpec((None, ATT_DH, ATT_KB), lambda b, h, qi, ki: (b * nk + ki, h, 0)),
            pl.BlockSpec((CHUNK, ATT_DH), lambda b, h, qi, ki: (0, h)),
            pl.BlockSpec((None, ATT_DH, CHUNK), lambda b, h, qi, ki: (0, h, 0)),
        ],
        out_specs=pl.BlockSpec((tq, qw), lambda b, h, qi, ki: (b * nq + qi, h)),
        scratch_shapes=[pltpu.VMEM((ATT_GROUP, 1, tq), F32),
                        pltpu.VMEM((ATT_GROUP, 1, tq), F32),
                        pltpu.VMEM((ATT_GROUP, ATT_DH, tq), F32)],
        compiler_params=pltpu.CompilerParams(
            dimension_semantics=("arbitrary",) * 4, vmem_limit_bytes=VMEM_LIMIT),
        name="attention_online",
    )(qt, ak, vt, ak_meta, vt_meta)


def _out_proj_kernel(x_ref, r_ref, a_ref, wr_ref, wa_ref, ln_ref, h_ref, u_ref):
    h = (x_ref[...]
         + jnp.dot(r_ref[...], wr_ref[...], preferred_element_type=F32)
         + jnp.dot(a_ref[...], wa_ref[...], preferred_element_type=F32))
    h_ref[...] = h
    ms = jnp.mean(h * h, axis=-1, keepdims=True)
    u_ref[...] = (h * lax.rsqrt(ms + EPS) * ln_ref[...]).astype(BF16)


def _out_proj(x, ret, att, w_out, ln2, *, tm):
    t = x.shape[0]
    row = lambda w: pl.BlockSpec((tm, w), lambda i: (i, 0))
    return pl.pallas_call(
        _out_proj_kernel,
        out_shape=(jax.ShapeDtypeStruct((t, D_MODEL), F32),
                   jax.ShapeDtypeStruct((t, D_MODEL), BF16)),
        grid=(t // tm,),
        in_specs=[
            row(D_MODEL), row(RET_V), row(ATT_Q),
            pl.BlockSpec((RET_V, D_MODEL), lambda i: (0, 0)),
            pl.BlockSpec((ATT_Q, D_MODEL), lambda i: (1, 0)),
            pl.BlockSpec((1, D_MODEL), lambda i: (0, 0)),
        ],
        out_specs=(row(D_MODEL), row(D_MODEL)),
        compiler_params=pltpu.CompilerParams(
            dimension_semantics=("arbitrary",), vmem_limit_bytes=VMEM_LIMIT),
        name="out_proj",
    )(x, ret, att, w_out, w_out, ln2)


def _mlp_kernel(u_ref, h_ref, wu_ref, wd_ref, fg_ref, o_ref):
    j = pl.program_id(1)
    last = pl.num_programs(1) - 1
    tm = u_ref.shape[0]

    def block(rows, first, final):
        a = jnp.dot(u_ref[rows, :], wu_ref[...], preferred_element_type=F32)
        a = jnp.square(jnp.maximum(a, 0.0)).astype(BF16)
        base = h_ref[rows, :] if first else o_ref[rows, :]
        h = base + jnp.dot(a, wd_ref[...], preferred_element_type=F32)
        if final:
            ms = jnp.mean(h * h, axis=-1, keepdims=True)
            h = h * lax.rsqrt(ms + EPS) * fg_ref[...]
        o_ref[rows, :] = h

    everything = slice(0, tm)
    pl.when(j == 0)(lambda: block(everything, True, False))
    pl.when((j > 0) & (j < last))(lambda: block(everything, False, False))

    @pl.when(j == last)
    def _():
        for r in range(2):
            block(slice(r * tm // 2, (r + 1) * tm // 2), False, True)


def _mlp(u, h, w_up, w_down, fg, *, tm, tf):
    t = u.shape[0]
    assert D_FF // tf >= 2
    return pl.pallas_call(
        _mlp_kernel,
        out_shape=jax.ShapeDtypeStruct((t, D_MODEL), F32),
        grid=(t // tm, D_FF // tf),
        in_specs=[
            pl.BlockSpec((tm, D_MODEL), lambda i, j: (i, 0)),
            pl.BlockSpec((tm, D_MODEL), lambda i, j: (i, 0)),
            pl.BlockSpec((D_MODEL, tf), lambda i, j: (0, j)),
            pl.BlockSpec((tf, D_MODEL), lambda i, j: (j, 0)),
            pl.BlockSpec((1, D_MODEL), lambda i, j: (0, 0)),
        ],
        out_specs=pl.BlockSpec((tm, D_MODEL), lambda i, j: (i, 0)),
        compiler_params=pltpu.CompilerParams(
            dimension_semantics=("arbitrary", "arbitrary"), vmem_limit_bytes=VMEM_LIMIT),
        name="mlp",
    )(u, h, w_up, w_down, fg)


def _rope_tables(seq):
    pairs = lambda t: jnp.repeat(t, 2, axis=-1)
    sign = jnp.tile(jnp.array([-1.0, 1.0], F32), 64)
    n_chunks = seq // CHUNK + 1
    freq_r = ROPE_THETA ** (-jnp.linspace(0.0, 1.0, RET_DK // 2, dtype=F32))
    ang_hi = (jnp.arange(n_chunks, dtype=F32) * float(CHUNK))[:, None] * freq_r[None]
    ang_lo = jnp.arange(CHUNK, dtype=F32)[:, None] * freq_r[None]
    rhi = jnp.stack([pairs(jnp.cos(ang_hi)), pairs(jnp.sin(ang_hi))])
    c_lo, s_lo = pairs(jnp.cos(ang_lo)), pairs(jnp.sin(ang_lo))
    rlo = jnp.stack([c_lo, s_lo, sign * c_lo, sign * s_lo])
    n_pair = ATT_DH // 4
    freq_a = ROPE_THETA ** (-jnp.arange(n_pair, dtype=F32) / n_pair)
    ang_row = jnp.arange(seq // GRID_W, dtype=F32)[:, None] * freq_a[None]
    ang_col = jnp.arange(GRID_W, dtype=F32)[:, None] * freq_a[None]
    in_row = lambda t: jnp.concatenate([pairs(t), jnp.zeros_like(pairs(t))], axis=-1)
    in_col = lambda t: jnp.concatenate([jnp.zeros_like(pairs(t)), pairs(t)], axis=-1)
    arow = jnp.stack([in_row(jnp.cos(ang_row)), sign * in_row(jnp.sin(ang_row))])
    acol = jnp.stack([in_col(jnp.cos(ang_col)), sign * in_col(jnp.sin(ang_col))])
    return rhi, rlo, arow, acol


def _project(x, wts, tabs, meta):
    batch, seq, _ = x.shape
    for tile in (IN_PROJ_TM, RET_STEP_CHUNKS * CHUNK, ATT_TQ, ATT_KB, OUT_PROJ_TM, MLP_TM):
        assert seq % tile == 0, (seq, tile)
    x2 = x.reshape(batch * seq, D_MODEL)
    proj, qt, ak, vt = _in_proj(x2, wts["ln1"], wts["w_in"], *tabs, wts["qg"], wts["kg"],
                                tm=IN_PROJ_TM, table_blocks=seq // IN_PROJ_TM)
    ret = _retention(proj, meta[0], wts["lgf"], wts["lgb"], batch=batch, seq=seq)
    return x2, ret, (qt, ak, vt)


def _attend(qkv, wts, meta, shape, bounded):
    batch, seq, _ = shape
    _, ak_meta, vt_meta = meta
    if bounded:
        return _attention_bounded(*qkv, ak_meta, vt_meta, wts["att_bound"],
                                  batch=batch, seq=seq, tq=ATT_TQ)
    return _attention_online(*qkv, ak_meta, vt_meta, batch=batch, seq=seq, tq=ATT_TQ)


def _finish(x2, ret, att, wts, shape):
    h1, u2 = _out_proj(x2, ret, att, wts["w_out"], wts["ln2"], tm=OUT_PROJ_TM)
    y = _mlp(u2, h1, wts["w_up"], wts["w_down"], wts["fg"], tm=MLP_TM, tf=MLP_TF)
    return y.reshape(shape)


def kernel(x_prompt, x_sample, meta_tokens, ln1_g, w_in, q_norm_g, k_norm_g, ret_log_decay_fwd,
           ret_log_decay_bwd, w_out, ln2_g, w_up, w_down, final_norm_g):
    assert w_in.shape[0] == 1, "meta-token residual stream is only skippable for a single layer"
    wts = {
        "ln1": ln1_g[0].reshape(1, D_MODEL),
        "w_in": w_in[0].astype(BF16),
        "qg": q_norm_g[0].reshape(1, ATT_DH),
        "kg": k_norm_g[0].reshape(1, ATT_DH),
        "lgf": ret_log_decay_fwd[0],
        "lgb": ret_log_decay_bwd[0],
        "w_out": w_out[0].astype(BF16),
        "ln2": ln2_g[0].reshape(1, D_MODEL),
        "w_up": w_up[0].astype(BF16),
        "w_down": w_down[0].astype(BF16),
        "fg": final_norm_g.reshape(1, D_MODEL),
    }
    wts["att_bound"] = (1.02 * LOG2E * ATT_DH ** 0.5 * jnp.max(jnp.abs(q_norm_g[0]))
                        * jnp.max(jnp.abs(k_norm_g[0]))).reshape(1).astype(F32)
    max_seq = max(x_prompt.shape[1], x_sample.shape[1])
    rhi, rlo, arow, acol = _rope_tables(max_seq)

    x_meta = jnp.concatenate(
        [jnp.zeros((CHUNK - N_META, D_MODEL), F32), meta_tokens.astype(F32)], axis=0)
    unrotated = jnp.stack([jnp.ones((CHUNK // GRID_W, LANES), F32),
                           jnp.zeros((CHUNK // GRID_W, LANES), F32)])
    proj_meta, _, ak_meta, vt_meta = _in_proj(
        x_meta, wts["ln1"], wts["w_in"], rhi[:, :1], rlo, unrotated, jnp.zeros_like(acol),
        wts["qg"], wts["kg"], tm=CHUNK, table_blocks=1)
    meta = (proj_meta, ak_meta, vt_meta)

    tabs = (rhi[:, 1:], rlo, arow, acol)
    xs = (x_prompt, x_sample)
    x2s, rets, qkvs = zip(*[_project(x, wts, tabs, meta) for x in xs])
    atts = lax.cond(
        wts["att_bound"][0] <= ATT_BOUND_MAX,
        lambda: tuple(_attend(qkv, wts, meta, x.shape, True) for qkv, x in zip(qkvs, xs)),
        lambda: tuple(_attend(qkv, wts, meta, x.shape, False) for qkv, x in zip(qkvs, xs)))
    return tuple(_finish(x2, ret, att, wts, x.shape)
                 for x2, ret, att, x in zip(x2s, rets, atts, xs))
```

```python
import jax
import jax.numpy as jnp
from jax import lax
from jax.experimental import pallas as pl
from jax.experimental.pallas import tpu as pltpu

F32 = jnp.float32
BF16 = jnp.bfloat16

D_MODEL = 2048
N_META = 16
GRID_W = 64
CHUNK = 128
RET_HEADS = 4
RET_DK = 128
RET_DV = 256
RET_QK = RET_HEADS * RET_DK
RET_V = RET_HEADS * RET_DV
ATT_DH = 128
ATT_HEADS = 8
ATT_KV_HEADS = 2
ATT_GROUP = ATT_HEADS // ATT_KV_HEADS
ATT_Q = ATT_HEADS * ATT_DH
ATT_KV = ATT_KV_HEADS * ATT_DH
D_FF = 4 * D_MODEL
IN_W = 2 * RET_QK + 2 * RET_V + ATT_Q + 2 * ATT_KV
ROPE_THETA = 10000.0
EPS = 1e-6
LOG2E = 1.4426950408889634

OFF_RQ = 0
OFF_RK = OFF_RQ + RET_QK
OFF_RV = OFF_RK + RET_QK
OFF_RG = OFF_RV + RET_V
OFF_AQ = OFF_RG + RET_V
OFF_AK = OFF_AQ + ATT_Q
OFF_AV = OFF_AK + ATT_KV

LANES = 128
V7X_VMEM_BYTES = 64 * 1024 * 1024
VMEM_LIMIT = V7X_VMEM_BYTES * 7 // 8

PROJ_TN = 512
N_RET_BLOCKS = OFF_AQ // PROJ_TN
IN_PROJ_TM = 2048
IN_PROJ_SUB_ROWS = 256
RET_STEP_CHUNKS = 8
ATT_TQ = 1024
ATT_SUB = 256
ATT_KB = 2048
OUT_PROJ_TM = 512
MLP_TM = 512
MLP_TF = 2048
ATT_BOUND_MAX = 60.0

_NT = (((1,), (1,)), ((), ()))
_TN = (((0,), (0,)), ((), ()))


def _rope_pairs(a, c, s):
    even = lax.broadcasted_iota(jnp.int32, (a.shape[0], LANES), 1) % 2 == 0
    outs = []
    for h in range(a.shape[1] // LANES):
        ah = a[:, h * LANES:(h + 1) * LANES]
        partner = jnp.where(even, pltpu.roll(ah, LANES - 1, 1), pltpu.roll(ah, 1, 1))
        outs.append(ah * c + partner * s)
    return outs[0] if len(outs) == 1 else jnp.concatenate(outs, axis=1)


def _head_norm(a, g):
    outs = []
    for h in range(a.shape[1] // LANES):
        ah = a[:, h * LANES:(h + 1) * LANES]
        ms = jnp.mean(ah * ah, axis=-1, keepdims=True)
        outs.append(ah * lax.rsqrt(ms + EPS) * g)
    return outs[0] if len(outs) == 1 else jnp.concatenate(outs, axis=1)


def _in_proj_kernel(x_hbm, ln_ref, w_ref, rhi_ref, rlo_ref, arow_ref, acol_ref, qg_ref, kg_ref,
                    o_ref, qt_ref, ak_ref, vt_ref, u_ref, x_ref, x_sem):
    j = pl.program_id(1)
    tm = u_ref.shape[0]
    sub = min(tm, IN_PROJ_SUB_ROWS)

    def ret_tables(rs):
        cs, ss = [], []
        for a in range(rs.start // CHUNK, rs.stop // CHUNK):
            hc, hs = rhi_ref[0, a:a + 1, :], rhi_ref[1, a:a + 1, :]
            cs.append(hc * rlo_ref[0] - hs * rlo_ref[1])
            ss.append(hs * rlo_ref[2] + hc * rlo_ref[3])
        return jnp.concatenate(cs, axis=0), jnp.concatenate(ss, axis=0)

    def axial_tables(rs):
        rows = range(rs.start // GRID_W, rs.stop // GRID_W)
        return (jnp.concatenate([arow_ref[0, g:g + 1, :] + acol_ref[0] for g in rows], axis=0),
                jnp.concatenate([arow_ref[1, g:g + 1, :] + acol_ref[1] for g in rows], axis=0))

    def run(epilogue, normalize=False):
        for r in range(tm // sub):
            rs = slice(r * sub, (r + 1) * sub)
            if normalize:
                x = x_ref[rs, :]
                ms = jnp.mean(x * x, axis=-1, keepdims=True)
                u_ref[rs, :] = (x * lax.rsqrt(ms + EPS) * ln_ref[...]).astype(BF16)
            acc = jnp.dot(u_ref[rs, :], w_ref[...], preferred_element_type=F32)
            epilogue(acc, rs)

    def ret_q(acc, rs):
        o_ref[rs, :] = _rope_pairs(acc, *ret_tables(rs)).astype(BF16)

    def ret_k(acc, rs):
        o_ref[rs, :] = (_rope_pairs(acc, *ret_tables(rs)) * (RET_DK ** -0.5)).astype(BF16)

    def plain(acc, rs):
        o_ref[rs, :] = acc.astype(BF16)

    def att_q(acc, rs):
        qn = _head_norm(acc, qg_ref[...])
        q = _rope_pairs(qn, *axial_tables(rs)) * (ATT_DH ** -0.5 * LOG2E)
        qt_ref[:, rs] = q.T.astype(BF16)

    def att_kv(acc, rs):
        kn = _head_norm(acc[:, :ATT_KV], kg_ref[...])
        ak_ref[rs, :] = _rope_pairs(kn, *axial_tables(rs)).astype(BF16)
        kb = vt_ref.shape[2]
        vt_ref[rs.start // kb, :, rs.start % kb:rs.start % kb + sub] = (
            acc[:, ATT_KV:].T.astype(BF16))

    i = pl.program_id(0)

    def x_copy(tile):
        return pltpu.make_async_copy(x_hbm.at[pl.ds(tile * tm, tm), :], x_ref, x_sem)

    @pl.when(j == OFF_RQ // PROJ_TN)
    def _():
        pl.when(i == 0)(lambda: x_copy(0).start())
        x_copy(i).wait()
        run(ret_q, normalize=True)

    @pl.when(j == OFF_RK // PROJ_TN)
    def _():
        pl.when(i + 1 < pl.num_programs(0))(lambda: x_copy(i + 1).start())
        run(ret_k)

    pl.when((j >= OFF_RV // PROJ_TN) & (j < N_RET_BLOCKS))(lambda: run(plain))
    pl.when((j >= OFF_AQ // PROJ_TN) & (j < OFF_AK // PROJ_TN))(lambda: run(att_q))
    pl.when(j == OFF_AK // PROJ_TN)(lambda: run(att_kv))


def _in_proj(x, ln1, w_in, rhi, rlo, arow, acol, qg, kg, *, tm, table_blocks):
    t = x.shape[0]
    kb = min(tm, ATT_KB)
    n_j = IN_W // PROJ_TN
    j_aq = OFF_AQ // PROJ_TN
    seq_tab = lambda a, n: pl.BlockSpec((a.shape[0], n, LANES), lambda i, j: (0, i % table_blocks, 0))
    whole = lambda a: pl.BlockSpec(a.shape, lambda i, j: (0, 0, 0))
    vec = lambda n: pl.BlockSpec((1, n), lambda i, j: (0, 0))
    return pl.pallas_call(
        _in_proj_kernel,
        out_shape=(jax.ShapeDtypeStruct((t, OFF_AQ), BF16),
                   jax.ShapeDtypeStruct((ATT_Q, t), BF16),
                   jax.ShapeDtypeStruct((t, ATT_KV), BF16),
                   jax.ShapeDtypeStruct((t // kb, ATT_KV, kb), BF16)),
        grid=(t // tm, n_j),
        in_specs=[
            pl.BlockSpec(memory_space=pl.ANY),
            vec(D_MODEL),
            pl.BlockSpec((D_MODEL, PROJ_TN), lambda i, j: (0, j)),
            seq_tab(rhi, tm // CHUNK), whole(rlo), seq_tab(arow, tm // GRID_W), whole(acol),
            vec(ATT_DH), vec(ATT_DH),
        ],
        out_specs=(
            pl.BlockSpec((tm, PROJ_TN), lambda i, j: (i, jnp.minimum(j, N_RET_BLOCKS - 1))),
            pl.BlockSpec((PROJ_TN, tm), lambda i, j: (jnp.clip(j - j_aq, 0, 1), i)),
            pl.BlockSpec((tm, ATT_KV), lambda i, j: (i, 0)),
            pl.BlockSpec((tm // kb, ATT_KV, kb), lambda i, j: (i, 0, 0)),
        ),
        scratch_shapes=[pltpu.VMEM((tm, D_MODEL), BF16), pltpu.VMEM((tm, D_MODEL), F32),
                        pltpu.SemaphoreType.DMA],
        compiler_params=pltpu.CompilerParams(
            dimension_semantics=("arbitrary", "arbitrary"), vmem_limit_bytes=VMEM_LIMIT),
        name="in_proj",
    )(x, ln1, w_in, rhi, rlo, arow, acol, qg, kg)


def _chunk_pos():
    return lax.broadcasted_iota(jnp.int32, (CHUNK, 1), 0).astype(F32)


def _ret_bwd_kernel(lgf_ref, lgb_ref, q_ref, k_ref, v_ref, part_ref, s_ref, d_ref):
    b = pl.program_id(0)
    i = pl.program_id(1)

    @pl.when((b == 0) & (i == 0))
    def _():
        r = lax.broadcasted_iota(jnp.int32, (CHUNK, CHUNK), 0)
        c = lax.broadcasted_iota(jnp.int32, (CHUNK, CHUNK), 1)
        diff = (r - c).astype(F32)
        for h in range(RET_HEADS):
            df = jnp.where(diff >= 0, jnp.exp(lgf_ref[h] * jnp.maximum(diff, 0.0)), 0.0)
            db = jnp.where(diff < 0, jnp.exp(lgb_ref[h] * jnp.maximum(-diff, 0.0)), 0.0)
            d_ref[h] = df + db

    @pl.when(i == 0)
    def _():
        s_ref[...] = jnp.zeros_like(s_ref)

    pos = _chunk_pos()
    n_chunks = q_ref.shape[0] // CHUNK
    heads = range(RET_HEADS)
    q_dec = [jnp.exp(lgb_ref[h] * (float(CHUNK) - pos)) for h in heads]
    k_dec = [jnp.exp(lgb_ref[h] * pos) for h in heads]
    c_dec = [jnp.exp(lgb_ref[h] * jnp.full((CHUNK, 1), float(CHUNK), F32)) for h in heads]
    state = [s_ref[h] for h in heads]
    for cc in reversed(range(n_chunks)):
        rs = slice(cc * CHUNK, (cc + 1) * CHUNK)
        q = [q_ref[rs, h * RET_DK:(h + 1) * RET_DK] for h in heads]
        k = [k_ref[rs, h * RET_DK:(h + 1) * RET_DK] for h in heads]
        v = [v_ref[rs, h * RET_DV:(h + 1) * RET_DV] for h in heads]
        sc = [lax.dot_general(q[h], k[h], _NT, preferred_element_type=F32) * d_ref[h]
              for h in heads]
        lhs = [jnp.concatenate([sc[h].astype(BF16),
                                (q[h].astype(F32) * q_dec[h]).astype(BF16)], axis=1)
               for h in heads]
        rhs = [jnp.concatenate([v[h], state[h].astype(BF16)], axis=0) for h in heads]
        for h in heads:
            part_ref[rs, h * RET_DV:(h + 1) * RET_DV] = jnp.dot(
                lhs[h], rhs[h], preferred_element_type=F32)
        kv = [lax.dot_general((k[h].astype(F32) * k_dec[h]).astype(BF16), v[h], _TN,
                              preferred_element_type=F32) for h in heads]
        state = [state[h] * c_dec[h] + kv[h] for h in heads]
    for h in heads:
        s_ref[h] = state[h]


def _ret_fwd_kernel(lgf_ref, q_ref, k_ref, v_ref, g_ref, km_ref, vm_ref, part_ref,
                    o_ref, s_ref):
    i = pl.program_id(1)
    pos = _chunk_pos()

    @pl.when(i == 0)
    def _():
        for h in range(RET_HEADS):
            k_dec = jnp.exp(lgf_ref[h] * (float(CHUNK - 1) - pos))
            kd = (km_ref[:, h * RET_DK:(h + 1) * RET_DK].astype(F32) * k_dec).astype(BF16)
            s_ref[h] = lax.dot_general(kd, vm_ref[:, h * RET_DV:(h + 1) * RET_DV], _TN,
                                       preferred_element_type=F32)

    n_chunks = q_ref.shape[0] // CHUNK
    heads = range(RET_HEADS)
    q_dec = [jnp.exp(lgf_ref[h] * (pos + 1.0)) for h in heads]
    k_dec = [jnp.exp(lgf_ref[h] * (float(CHUNK - 1) - pos)) for h in heads]
    c_dec = [jnp.exp(lgf_ref[h] * jnp.full((CHUNK, 1), float(CHUNK), F32)) for h in heads]
    state = [s_ref[h] for h in heads]
    for cc in range(n_chunks):
        rs = slice(cc * CHUNK, (cc + 1) * CHUNK)
        cols = [slice(h * RET_DV, (h + 1) * RET_DV) for h in heads]
        q = [q_ref[rs, h * RET_DK:(h + 1) * RET_DK] for h in heads]
        k = [k_ref[rs, h * RET_DK:(h + 1) * RET_DK] for h in heads]
        v = [v_ref[rs, cols[h]] for h in heads]
        cross = [jnp.dot(q[h], state[h].astype(BF16), preferred_element_type=F32) * q_dec[h]
                 for h in heads]
        kv = [lax.dot_general((k[h].astype(F32) * k_dec[h]).astype(BF16), v[h], _TN,
                              preferred_element_type=F32) for h in heads]
        for h in heads:
            tot = part_ref[rs, cols[h]] + cross[h]
            ms = jnp.mean(tot * tot, axis=-1, keepdims=True)
            g = g_ref[rs, cols[h]].astype(F32)
            gate = g / (1.0 + jnp.exp(-g))
            o_ref[rs, cols[h]] = (tot * lax.rsqrt(ms + EPS) * gate).astype(BF16)
        state = [state[h] * c_dec[h] + kv[h] for h in heads]
    for h in heads:
        s_ref[h] = state[h]


def _retention(proj, proj_meta, lgf, lgb, *, batch, seq):
    t = proj.shape[0]
    rows_blk = RET_STEP_CHUNKS * CHUNK
    n = seq // rows_blk
    smem = pl.BlockSpec(memory_space=pltpu.SMEM)
    params = pltpu.CompilerParams(dimension_semantics=("arbitrary", "arbitrary"),
                                  vmem_limit_bytes=VMEM_LIMIT)

    def rows_rev(b, i):
        return b * n + (n - 1 - i)

    part = pl.pallas_call(
        _ret_bwd_kernel,
        out_shape=jax.ShapeDtypeStruct((t, RET_V), F32),
        grid=(batch, n),
        in_specs=[
            smem, smem,
            pl.BlockSpec((rows_blk, RET_QK), lambda b, i: (rows_rev(b, i), OFF_RQ // RET_QK)),
            pl.BlockSpec((rows_blk, RET_QK), lambda b, i: (rows_rev(b, i), OFF_RK // RET_QK)),
            pl.BlockSpec((rows_blk, RET_V), lambda b, i: (rows_rev(b, i), OFF_RV // RET_V)),
        ],
        out_specs=pl.BlockSpec((rows_blk, RET_V), lambda b, i: (rows_rev(b, i), 0)),
        scratch_shapes=[pltpu.VMEM((RET_HEADS, RET_DK, RET_DV), F32),
                        pltpu.VMEM((RET_HEADS, CHUNK, CHUNK), F32)],
        compiler_params=params,
        name="retention_bwd",
    )(lgf, lgb, proj, proj, proj)

    def rows(b, i):
        return b * n + i

    return pl.pallas_call(
        _ret_fwd_kernel,
        out_shape=jax.ShapeDtypeStruct((t, RET_V), BF16),
        grid=(batch, n),
        in_specs=[
            smem,
            pl.BlockSpec((rows_blk, RET_QK), lambda b, i: (rows(b, i), OFF_RQ // RET_QK)),
            pl.BlockSpec((rows_blk, RET_QK), lambda b, i: (rows(b, i), OFF_RK // RET_QK)),
            pl.BlockSpec((rows_blk, RET_V), lambda b, i: (rows(b, i), OFF_RV // RET_V)),
            pl.BlockSpec((rows_blk, RET_V), lambda b, i: (rows(b, i), OFF_RG // RET_V)),
            pl.BlockSpec((CHUNK, RET_QK), lambda b, i: (0, OFF_RK // RET_QK)),
            pl.BlockSpec((CHUNK, RET_V), lambda b, i: (0, OFF_RV // RET_V)),
            pl.BlockSpec((rows_blk, RET_V), lambda b, i: (rows(b, i), 0)),
        ],
        out_specs=pl.BlockSpec((rows_blk, RET_V), lambda b, i: (rows(b, i), 0)),
        scratch_shapes=[pltpu.VMEM((RET_HEADS, RET_DK, RET_DV), F32)],
        compiler_params=params,
        name="retention_fwd",
    )(lgf, proj, proj, proj, proj, proj_meta, proj_meta, part)


def _attn_subtile(k, vt, mask, qt_ref, m_ref, l_ref, acc_ref, g, c):
    qs = slice(c * ATT_SUB, (c + 1) * ATT_SUB)
    st = jnp.dot(k, qt_ref[g * ATT_DH:(g + 1) * ATT_DH, qs], preferred_element_type=F32)
    if mask is not None:
        st = jnp.where(mask, st, -jnp.inf)
    m_prev = m_ref[g, :, qs]
    m_new = jnp.maximum(m_prev, jnp.max(st, axis=0, keepdims=True))
    alpha = jnp.exp2(m_prev - m_new)
    p = jnp.exp2(st - m_new)
    l_ref[g, :, qs] = alpha * l_ref[g, :, qs] + jnp.sum(p, axis=0, keepdims=True)
    acc_ref[g, :, qs] = alpha * acc_ref[g, :, qs] + jnp.dot(
        vt, p.astype(BF16), preferred_element_type=F32)
    m_ref[g, :, qs] = m_new


def _attn_kernel(qt_ref, k_ref, vt_ref, km_ref, vtm_ref, o_ref, m_ref, l_ref, acc_ref):
    ki = pl.program_id(3)
    tq = qt_ref.shape[1]
    tk = k_ref.shape[0]

    @pl.when(ki == 0)
    def _():
        m_ref[...] = jnp.full_like(m_ref, -jnp.inf)
        l_ref[...] = jnp.zeros_like(l_ref)
        acc_ref[...] = jnp.zeros_like(acc_ref)
        is_meta = lax.broadcasted_iota(jnp.int32, (CHUNK, ATT_SUB), 0) >= CHUNK - N_META
        for g in range(ATT_GROUP):
            for c in range(tq // ATT_SUB):
                _attn_subtile(km_ref[...], vtm_ref[...], is_meta, qt_ref, m_ref, l_ref, acc_ref,
                              g, c)

    for r in range(tk // ATT_SUB):
        ks = slice(r * ATT_SUB, (r + 1) * ATT_SUB)
        for g in range(ATT_GROUP):
            for c in range(tq // ATT_SUB):
                _attn_subtile(k_ref[ks, :], vt_ref[:, ks], None, qt_ref, m_ref, l_ref, acc_ref,
                              g, c)

    @pl.when(ki == pl.num_programs(3) - 1)
    def _():
        for g in range(ATT_GROUP):
            o = acc_ref[g] / l_ref[g]
            o_ref[:, g * ATT_DH:(g + 1) * ATT_DH] = o.T.astype(BF16)


def _attn_bounded_kernel(bound_ref, qt_ref, k_ref, vt_ref, km_ref, vtm_ref, o_ref, l_ref, acc_ref):
    tq = qt_ref.shape[1]
    bound = bound_ref[0]
    subs = [slice(c * ATT_SUB, (c + 1) * ATT_SUB) for c in range(tq // ATT_SUB)]

    def update(k, vt, heads, first):
        tiles = [(g, qs) for g in heads for qs in subs]
        sts = [jnp.dot(k, qt_ref[g * ATT_DH:(g + 1) * ATT_DH, qs], preferred_element_type=F32)
               for g, qs in tiles]
        ps = [jnp.exp2(st - bound) for st in sts]
        for (g, qs), p in zip(tiles, ps):
            l = p.reshape(-1, 8, ATT_SUB).sum(axis=0)
            l_ref[g, :, qs] = l if first else l_ref[g, :, qs] + l
        for (g, qs), p in zip(tiles, ps):
            a = jnp.dot(vt, p.astype(BF16), preferred_element_type=F32)
            acc_ref[g, :, qs] = a if first else acc_ref[g, :, qs] + a

    update(km_ref[...], vtm_ref[...], range(ATT_GROUP), True)

    def key_block(kb, carry):
        k = k_ref[pl.ds(pl.multiple_of(kb * ATT_KB, ATT_KB), ATT_KB), :]
        vt = vt_ref[kb]
        update(k, vt, range(ATT_GROUP), False)
        return carry

    lax.fori_loop(0, vt_ref.shape[0], key_block, 0)

    for g in range(ATT_GROUP):
        o = acc_ref[g] / jnp.sum(l_ref[g], axis=0, keepdims=True)
        o_ref[:, g * ATT_DH:(g + 1) * ATT_DH] = o.T.astype(BF16)


def _attention_bounded(qt, ak, vt, ak_meta, vt_meta, bound, *, batch, seq, tq):
    t = ak.shape[0]
    nq = seq // tq
    nkb = seq // ATT_KB
    qw = ATT_GROUP * ATT_DH
    km = ak_meta[CHUNK - N_META:]
    vtm = vt_meta[0][:, CHUNK - N_META:]
    return pl.pallas_call(
        _attn_bounded_kernel,
        out_shape=jax.ShapeDtypeStruct((t, ATT_Q), BF16),
        grid=(batch, ATT_KV_HEADS, nq),
        in_specs=[
            pl.BlockSpec(memory_space=pltpu.SMEM),
            pl.BlockSpec((qw, tq), lambda b, h, qi: (h, b * nq + qi)),
            pl.BlockSpec((seq, ATT_DH), lambda b, h, qi: (b, h)),
            pl.BlockSpec((nkb, ATT_DH, ATT_KB), lambda b, h, qi: (b, h, 0)),
            pl.BlockSpec((N_META, ATT_DH), lambda b, h, qi: (0, h)),
            pl.BlockSpec((ATT_DH, N_META), lambda b, h, qi: (h, 0)),
        ],
        out_specs=pl.BlockSpec((tq, qw), lambda b, h, qi: (b * nq + qi, h)),
        scratch_shapes=[pltpu.VMEM((ATT_GROUP, 8, tq), F32),
                        pltpu.VMEM((ATT_GROUP, ATT_DH, tq), F32)],
        compiler_params=pltpu.CompilerParams(
            dimension_semantics=("arbitrary",) * 3, vmem_limit_bytes=VMEM_LIMIT),
        name="attention_bounded",
    )(bound, qt, ak, vt, km, vtm)


def _attention_online(qt, ak, vt, ak_meta, vt_meta, *, batch, seq, tq):
    t = ak.shape[0]
    nq = seq // tq
    nk = seq // ATT_KB
    qw = ATT_GROUP * ATT_DH
    return pl.pallas_call(
        _attn_kernel,
        out_shape=jax.ShapeDtypeStruct((t, ATT_Q), BF16),
        grid=(batch, ATT_KV_HEADS, nq, nk),
        in_specs=[
            pl.BlockSpec((qw, tq), lambda b, h, qi, ki: (h, b * nq + qi)),
            pl.BlockSpec((ATT_KB, ATT_DH), lambda b, h, qi, ki: (b * nk + ki, h)),
            pl.BlockSpec((None, ATT_DH, ATT_KB), lambda b, h, qi, ki: (b * nk + ki, h, 0)),
            pl.BlockSpec((CHUNK, ATT_DH), lambda b, h, qi, ki: (0, h)),
            pl.BlockSpec((None, ATT_DH, CHUNK), lambda b, h, qi, ki: (0, h, 0)),
        ],
        out_specs=pl.BlockSpec((tq, qw), lambda b, h, qi, ki: (b * nq + qi, h)),
        scratch_shapes=[pltpu.VMEM((ATT_GROUP, 1, tq), F32),
                        pltpu.VMEM((ATT_GROUP, 1, tq), F32),
                        pltpu.VMEM((ATT_GROUP, ATT_DH, tq), F32)],
        compiler_params=pltpu.CompilerParams(
            dimension_semantics=("arbitrary",) * 4, vmem_limit_bytes=VMEM_LIMIT),
        name="attention_online",
    )(qt, ak, vt, ak_meta, vt_meta)


def _out_proj_kernel(x_ref, r_ref, a_ref, wr_ref, wa_ref, h_ref):
    h = (x_ref[...]
         + jnp.dot(r_ref[...], wr_ref[...], preferred_element_type=F32)
         + jnp.dot(a_ref[...], wa_ref[...], preferred_element_type=F32))
    h_ref[...] = h


def _out_proj(x, ret, att, w_out, *, tm):
    t = x.shape[0]
    row = lambda w: pl.BlockSpec((tm, w), lambda i: (i, 0))
    return pl.pallas_call(
        _out_proj_kernel,
        out_shape=jax.ShapeDtypeStruct((t, D_MODEL), F32),
        grid=(t // tm,),
        in_specs=[
            row(D_MODEL), row(RET_V), row(ATT_Q),
            pl.BlockSpec((RET_V, D_MODEL), lambda i: (0, 0)),
            pl.BlockSpec((ATT_Q, D_MODEL), lambda i: (1, 0)),
        ],
        out_specs=row(D_MODEL),
        compiler_params=pltpu.CompilerParams(
            dimension_semantics=("arbitrary",), vmem_limit_bytes=VMEM_LIMIT),
        name="out_proj",
    )(x, ret, att, w_out, w_out)


def _mlp_kernel(h_ref, wu_ref, wd_ref, ln_ref, fg_ref, o_ref, u_ref):
    j = pl.program_id(1)
    last = pl.num_programs(1) - 1
    tm = u_ref.shape[0]

    def block(rows, first, final):
        if first:
            h1 = h_ref[rows, :]
            ms1 = jnp.mean(h1 * h1, axis=-1, keepdims=True)
            u_ref[rows, :] = (h1 * lax.rsqrt(ms1 + EPS) * ln_ref[...]).astype(BF16)
        a = jnp.dot(u_ref[rows, :], wu_ref[...], preferred_element_type=F32)
        a = jnp.square(jnp.maximum(a, 0.0)).astype(BF16)
        base = h_ref[rows, :] if first else o_ref[rows, :]
        h = base + jnp.dot(a, wd_ref[...], preferred_element_type=F32)
        if final:
            ms = jnp.mean(h * h, axis=-1, keepdims=True)
            h = h * lax.rsqrt(ms + EPS) * fg_ref[...]
        o_ref[rows, :] = h

    everything = slice(0, tm)
    pl.when(j == 0)(lambda: block(everything, True, False))
    pl.when((j > 0) & (j < last))(lambda: block(everything, False, False))

    @pl.when(j == last)
    def _():
        for r in range(2):
            block(slice(r * tm // 2, (r + 1) * tm // 2), False, True)


def _mlp(h, w_up, w_down, ln2, fg, *, tm, tf):
    t = h.shape[0]
    assert D_FF // tf >= 2
    return pl.pallas_call(
        _mlp_kernel,
        out_shape=jax.ShapeDtypeStruct((t, D_MODEL), F32),
        grid=(t // tm, D_FF // tf),
        in_specs=[
            pl.BlockSpec((tm, D_MODEL), lambda i, j: (i, 0)),
            pl.BlockSpec((D_MODEL, tf), lambda i, j: (0, j)),
            pl.BlockSpec((tf, D_MODEL), lambda i, j: (j, 0)),
            pl.BlockSpec((1, D_MODEL), lambda i, j: (0, 0)),
            pl.BlockSpec((1, D_MODEL), lambda i, j: (0, 0)),
        ],
        out_specs=pl.BlockSpec((tm, D_MODEL), lambda i, j: (i, 0)),
        scratch_shapes=[pltpu.VMEM((tm, D_MODEL), BF16)],
        compiler_params=pltpu.CompilerParams(
            dimension_semantics=("arbitrary", "arbitrary"), vmem_limit_bytes=VMEM_LIMIT),
        name="mlp",
    )(h, w_up, w_down, ln2, fg)


def _rope_tables(seq):
    pairs = lambda t: jnp.repeat(t, 2, axis=-1)
    sign = jnp.tile(jnp.array([-1.0, 1.0], F32), 64)
    n_chunks = seq // CHUNK + 1
    freq_r = ROPE_THETA ** (-jnp.linspace(0.0, 1.0, RET_DK // 2, dtype=F32))
    ang_hi = (jnp.arange(n_chunks, dtype=F32) * float(CHUNK))[:, None] * freq_r[None]
    ang_lo = jnp.arange(CHUNK, dtype=F32)[:, None] * freq_r[None]
    rhi = jnp.stack([pairs(jnp.cos(ang_hi)), pairs(jnp.sin(ang_hi))])
    c_lo, s_lo = pairs(jnp.cos(ang_lo)), pairs(jnp.sin(ang_lo))
    rlo = jnp.stack([c_lo, s_lo, sign * c_lo, sign * s_lo])
    n_pair = ATT_DH // 4
    freq_a = ROPE_THETA ** (-jnp.arange(n_pair, dtype=F32) / n_pair)
    ang_row = jnp.arange(seq // GRID_W, dtype=F32)[:, None] * freq_a[None]
    ang_col = jnp.arange(GRID_W, dtype=F32)[:, None] * freq_a[None]
    in_row = lambda t: jnp.concatenate([pairs(t), jnp.zeros_like(pairs(t))], axis=-1)
    in_col = lambda t: jnp.concatenate([jnp.zeros_like(pairs(t)), pairs(t)], axis=-1)
    arow = jnp.stack([in_row(jnp.cos(ang_row)), sign * in_row(jnp.sin(ang_row))])
    acol = jnp.stack([in_col(jnp.cos(ang_col)), sign * in_col(jnp.sin(ang_col))])
    return rhi, rlo, arow, acol


def _project(x, wts, tabs, meta):
    batch, seq, _ = x.shape
    for tile in (IN_PROJ_TM, RET_STEP_CHUNKS * CHUNK, ATT_TQ, ATT_KB, OUT_PROJ_TM, MLP_TM):
        assert seq % tile == 0, (seq, tile)
    x2 = x.reshape(batch * seq, D_MODEL)
    proj, qt, ak, vt = _in_proj(x2, wts["ln1"], wts["w_in"], *tabs, wts["qg"], wts["kg"],
                                tm=IN_PROJ_TM, table_blocks=seq // IN_PROJ_TM)
    ret = _retention(proj, meta[0], wts["lgf"], wts["lgb"], batch=batch, seq=seq)
    return x2, ret, (qt, ak, vt)


def _attend(qkv, wts, meta, shape, bounded):
    batch, seq, _ = shape
    _, ak_meta, vt_meta = meta
    if bounded:
        return _attention_bounded(*qkv, ak_meta, vt_meta, wts["att_bound"],
                                  batch=batch, seq=seq, tq=ATT_TQ)
    return _attention_online(*qkv, ak_meta, vt_meta, batch=batch, seq=seq, tq=ATT_TQ)


def _finish(x2, ret, att, wts, shape):
    h1 = _out_proj(x2, ret, att, wts["w_out"], tm=OUT_PROJ_TM)
    y = _mlp(h1, wts["w_up"], wts["w_down"], wts["ln2"], wts["fg"], tm=MLP_TM, tf=MLP_TF)
    return y.reshape(shape)


def kernel(x_prompt, x_sample, meta_tokens, ln1_g, w_in, q_norm_g, k_norm_g, ret_log_decay_fwd,
           ret_log_decay_bwd, w_out, ln2_g, w_up, w_down, final_norm_g):
    assert w_in.shape[0] == 1, "meta-token residual stream is only skippable for a single layer"
    wts = {
        "ln1": ln1_g[0].reshape(1, D_MODEL),
        "w_in": w_in[0].astype(BF16),
        "qg": q_norm_g[0].reshape(1, ATT_DH),
        "kg": k_norm_g[0].reshape(1, ATT_DH),
        "lgf": ret_log_decay_fwd[0],
        "lgb": ret_log_decay_bwd[0],
        "w_out": w_out[0].astype(BF16),
        "ln2": ln2_g[0].reshape(1, D_MODEL),
        "w_up": w_up[0].astype(BF16),
        "w_down": w_down[0].astype(BF16),
        "fg": final_norm_g.reshape(1, D_MODEL),
    }
    wts["att_bound"] = (1.02 * LOG2E * ATT_DH ** 0.5 * jnp.max(jnp.abs(q_norm_g[0]))
                        * jnp.max(jnp.abs(k_norm_g[0]))).reshape(1).astype(F32)
    max_seq = max(x_prompt.shape[1], x_sample.shape[1])
    rhi, rlo, arow, acol = _rope_tables(max_seq)

    x_meta = jnp.concatenate(
        [jnp.zeros((CHUNK - N_META, D_MODEL), F32), meta_tokens.astype(F32)], axis=0)
    unrotated = jnp.stack([jnp.ones((CHUNK // GRID_W, LANES), F32),
                           jnp.zeros((CHUNK // GRID_W, LANES), F32)])
    proj_meta, _, ak_meta, vt_meta = _in_proj(
        x_meta, wts["ln1"], wts["w_in"], rhi[:, :1], rlo, unrotated, jnp.zeros_like(acol),
        wts["qg"], wts["kg"], tm=CHUNK, table_blocks=1)
    meta = (proj_meta, ak_meta, vt_meta)

    tabs = (rhi[:, 1:], rlo, arow, acol)
    xs = (x_prompt, x_sample)
    x2s, rets, qkvs = zip(*[_project(x, wts, tabs, meta) for x in xs])
    atts = lax.cond(
        wts["att_bound"][0] <= ATT_BOUND_MAX,
        lambda: tuple(_attend(qkv, wts, meta, x.shape, True) for qkv, x in zip(qkvs, xs)),
        lambda: tuple(_attend(qkv, wts, meta, x.shape, False) for qkv, x in zip(qkvs, xs)))
    return tuple(_finish(x2, ret, att, wts, x.shape)
                 for x2, ret, att, x in zip(x2s, rets, atts, xs))
```
